```python
import math
import jax, jax.numpy as jnp
from jax import lax
import numpy as np

D_MODEL = 1024
BATCH = 4
SEQ = 8192
DEPTH = 4

NORM_EPS = 1e-6
GLA_HEADS = 4
GLA_DK = 64
GLA_DV = 128
GLA_GATE_RANK = 16
GLA_GATE_TEMP = 16.0
GLA_CHUNK = 64
GLA_KW = GLA_HEADS * GLA_DK
GLA_VW = GLA_HEADS * GLA_DV
HGRN_HEADS = 4
HGRN_DIM = 128
HGRN_CHUNK = 64
HGRN_W = HGRN_HEADS * HGRN_DIM
SSM_D_INNER = D_MODEL
SSM_HEAD_DIM = 64
SSM_HEADS = SSM_D_INNER // SSM_HEAD_DIM
SSM_GROUPS = 4
SSM_STATE = 128
SSM_CONV = 4
SSM_CHUNK = 128
SSM_CONV_DIM = SSM_D_INNER + 2 * SSM_GROUPS * SSM_STATE
N_BRANCHES = 3
MOE_GROUPS = 4
MOE_EXPERTS_PER_GROUP = 8
MOE_EXPERTS = MOE_GROUPS * MOE_EXPERTS_PER_GROUP
MOE_TOP_K = 2
MOE_FF = 512
MOE_BLOCK = 128
IN_SPLITS = (GLA_KW, GLA_KW, GLA_VW, GLA_GATE_RANK, GLA_VW,
             HGRN_W, HGRN_W, HGRN_W, HGRN_W,
             SSM_D_INNER, SSM_CONV_DIM, SSM_HEADS,
             N_BRANCHES * D_MODEL)
IN_PROJ_WIDTH = (2 * GLA_KW + 2 * GLA_VW + GLA_GATE_RANK + 4 * HGRN_W
                 + SSM_D_INNER + SSM_CONV_DIM + SSM_HEADS + N_BRANCHES * D_MODEL)

kernel_name = "hybrid_gla_hgrn2_ssd_hmoe"


def rms_norm(x, w):
    xf = x.astype(jnp.float32)
    y = xf * lax.rsqrt(jnp.mean(xf * xf, axis=-1, keepdims=True) + NORM_EPS)
    return (y * w.astype(jnp.float32)).astype(x.dtype)


def _to_chunks(t, chunk):
    b, t_len, h, d = t.shape
    return jnp.transpose(t.reshape(b, t_len // chunk, chunk, h, d), (1, 0, 3, 2, 4))


def _from_chunks(t):
    n, b, h, c, d = t.shape
    return jnp.transpose(t, (1, 0, 3, 2, 4)).reshape(b, n * c, h, d)


def chunked_gated_linear_attention(q, k, v, log_decay, chunk):
    bsz, _, h, dk = q.shape
    dv = v.shape[-1]
    qc, kc, vc, gc = (_to_chunks(t.astype(jnp.float32), chunk) for t in (q, k, v, log_decay))
    causal = jnp.tril(jnp.ones((chunk, chunk), dtype=bool))[:, :, None]

    def step(state, inp):
        qi, ki, vi, gi = inp
        b = jnp.cumsum(gi, axis=2)
        o_inter = jnp.einsum('bhcd,bhde->bhce', qi * jnp.exp(b), state)
        rel = b[:, :, :, None, :] - b[:, :, None, :, :]
        rel_decay = jnp.where(causal, jnp.exp(jnp.where(causal, rel, 0.0)), 0.0)
        scores = jnp.einsum('bhid,bhjd,bhijd->bhij', qi, ki, rel_decay)
        o_intra = jnp.einsum('bhij,bhje->bhie', scores, vi)
        b_last = b[:, :, -1, :]
        k_to_end = ki * jnp.exp(b_last[:, :, None, :] - b)
        state = state * jnp.exp(b_last)[..., None] + jnp.einsum('bhcd,bhce->bhde', k_to_end, vi)
        return state, o_inter + o_intra

    s0 = jnp.zeros((bsz, h, dk, dv), jnp.float32)
    _, o = lax.scan(step, s0, (qc, kc, vc, gc))
    return _from_chunks(o)


def ssd_chunked_scan(x, dt, a_neg, bm, cm, chunk):
    bsz, t_len, h, p = x.shape
    g, n_state = bm.shape[2], bm.shape[3]
    rep = h // g
    n = t_len // chunk
    xc = jnp.transpose(x.reshape(bsz, n, chunk, g, rep, p), (1, 0, 3, 4, 2, 5))
    dtc = jnp.transpose(dt.reshape(bsz, n, chunk, g, rep), (1, 0, 3, 4, 2))
    bc = jnp.transpose(bm.reshape(bsz, n, chunk, g, n_state), (1, 0, 3, 2, 4))
    cc = jnp.transpose(cm.reshape(bsz, n, chunk, g, n_state), (1, 0, 3, 2, 4))
    ac = dtc * a_neg.reshape(g, rep)[None, None, :, :, None]
    causal = jnp.tril(jnp.ones((chunk, chunk), dtype=bool))

    def step(state, inp):
        xi, dti, ai, bi, ci = inp
        b = jnp.cumsum(ai, axis=-1)
        seg = b[..., :, None] - b[..., None, :]
        decay = jnp.where(causal, jnp.exp(jnp.where(causal, seg, 0.0)), 0.0)
        cb = jnp.einsum('bgin,bgjn->bgij', ci, bi)
        w = decay * cb[:, :, None] * dti[..., None, :]
        y = jnp.einsum('bgrij,bgrjp->bgrip', w, xi)
        y = y + jnp.einsum('bgin,bgrpn->bgrip', ci, state) * jnp.exp(b)[..., None]
        b_last = b[..., -1:]
        wts = jnp.exp(b_last - b) * dti
        state = state * jnp.exp(b_last)[..., None] + jnp.einsum('bgrc,bgrcp,bgcn->bgrpn', wts, xi, bi)
        return state, y

    s0 = jnp.zeros((bsz, g, rep, p, n_state), jnp.float32)
    _, y = lax.scan(step, s0, (xc, dtc, ac, bc, cc))
    return jnp.transpose(y, (1, 0, 4, 2, 3, 5)).reshape(bsz, t_len, h, p)


def causal_depthwise_conv(u, w, b):
    c = u.shape[-1]
    out = lax.conv_general_dilated(
        u, w[:, None, :].astype(u.dtype), window_strides=(1,),
        padding=((w.shape[0] - 1, 0),), dimension_numbers=('NWC', 'WIO', 'NWC'),
        feature_group_count=c)
    return out + b.astype(u.dtype)


def hybrid_mixer(h, w_in, gla_gk_up, gla_gk_bias, gla_norm_w, lb, hgrn_norm_w,
                 ssm_conv_w, ssm_conv_b, ssm_dt_bias, ssm_a_log, ssm_d, ssm_norm_w,
                 w_br_gla, w_br_hgrn, w_br_ssm, w_out):
    f32 = jnp.float32
    dtype = h.dtype
    bsz, t_len, _ = h.shape
    proj = h @ w_in
    offs = np.cumsum(IN_SPLITS)[:-1].tolist()
    (gq, gk, gv, g_low, gr, hq, hf, hi, hg, sz, sxbc, sdt, gates) = jnp.split(proj, offs, axis=-1)

    q = gq.astype(f32).reshape(bsz, t_len, GLA_HEADS, GLA_DK) * GLA_DK ** -0.5
    k = gk.astype(f32).reshape(bsz, t_len, GLA_HEADS, GLA_DK)
    v = gv.astype(f32).reshape(bsz, t_len, GLA_HEADS, GLA_DV)
    gate_logit = (g_low @ gla_gk_up + gla_gk_bias).astype(f32)
    log_a = (jax.nn.log_sigmoid(gate_logit) / GLA_GATE_TEMP).reshape(bsz, t_len, GLA_HEADS, GLA_DK)
    o = chunked_gated_linear_attention(q, k, v, log_a, GLA_CHUNK)
    o = rms_norm(o, gla_norm_w.reshape(GLA_HEADS, GLA_DV)).reshape(bsz, t_len, GLA_VW)
    y_gla = (o * jax.nn.silu(gr.astype(f32))).astype(dtype)

    lbh = lb.reshape(HGRN_HEADS, HGRN_DIM)
    q = jax.nn.silu(hq.astype(f32)).reshape(bsz, t_len, HGRN_HEADS, HGRN_DIM) * HGRN_DIM ** -0.5
    f_raw = hf.astype(f32).reshape(bsz, t_len, HGRN_HEADS, HGRN_DIM)
    log_f = jnp.logaddexp(jnp.log(lbh), jnp.log1p(-lbh) + jax.nn.log_sigmoid(f_raw))
    k = (1.0 - lbh) * jax.nn.sigmoid(-f_raw)
    v = hi.astype(f32).reshape(bsz, t_len, HGRN_HEADS, HGRN_DIM)
    o = chunked_gated_linear_attention(q, k, v, log_f, HGRN_CHUNK)
    o = rms_norm(o, hgrn_norm_w.reshape(HGRN_HEADS, HGRN_DIM)).reshape(bsz, t_len, HGRN_W)
    y_hgrn = (o * jax.nn.silu(hg.astype(f32))).astype(dtype)

    xbc = jax.nn.silu(causal_depthwise_conv(sxbc, ssm_conv_w, ssm_conv_b))
    xs, bm, cm = jnp.split(xbc, [SSM_D_INNER, SSM_D_INNER + SSM_GROUPS * SSM_STATE], axis=-1)
    xs = xs.astype(f32).reshape(bsz, t_len, SSM_HEADS, SSM_HEAD_DIM)
    bm = bm.astype(f32).reshape(bsz, t_len, SSM_GROUPS, SSM_STATE)
    cm = cm.astype(f32).reshape(bsz, t_len, SSM_GROUPS, SSM_STATE)
    dt = jax.nn.softplus(sdt.astype(f32) + ssm_dt_bias.astype(f32))
    a_neg = -jnp.exp(ssm_a_log.astype(f32))
    y = ssd_chunked_scan(xs, dt, a_neg, bm, cm, SSM_CHUNK)
    y = y + ssm_d.astype(f32)[:, None] * xs
    y = y.reshape(bsz, t_len, SSM_D_INNER) * jax.nn.silu(sz.astype(f32))
    y = rms_norm(y.reshape(bsz, t_len, SSM_GROUPS, SSM_D_INNER // SSM_GROUPS),
                 ssm_norm_w.reshape(SSM_GROUPS, SSM_D_INNER // SSM_GROUPS))
    y_ssm = y.reshape(bsz, t_len, SSM_D_INNER).astype(dtype)

    g = jax.nn.sigmoid(gates.astype(f32)).reshape(bsz, t_len, N_BRANCHES, D_MODEL).astype(dtype)
    merged = (g[:, :, 0] * (y_gla @ w_br_gla)
              + g[:, :, 1] * (y_hgrn @ w_br_hgrn)
              + g[:, :, 2] * (y_ssm @ w_br_ssm))
    return merged @ w_out


def grouped_expert_ffn(tokens, expert_ids, w_gate, w_up, w_down):
    n, d = tokens.shape
    e_count = w_gate.shape[0]
    flat_e = expert_ids.reshape(-1)
    n_assign = flat_e.shape[0]
    order = jnp.argsort(flat_e)
    sorted_e = flat_e[order]
    sorted_tok = order // MOE_TOP_K
    counts = jnp.bincount(flat_e, length=e_count)
    starts = jnp.cumsum(counts) - counts
    padded = (counts + MOE_BLOCK - 1) // MOE_BLOCK * MOE_BLOCK
    padded_ends = jnp.cumsum(padded)
    padded_starts = padded_ends - padded
    dest = padded_starts[sorted_e] + jnp.arange(n_assign) - starts[sorted_e]
    rows = n_assign + e_count * MOE_BLOCK
    n_blocks = rows // MOE_BLOCK
    buf_tok = jnp.full((rows,), n, jnp.int32).at[dest].set(sorted_tok.astype(jnp.int32))
    tok_pad = jnp.concatenate([tokens, jnp.zeros((1, d), tokens.dtype)], axis=0)
    xb = tok_pad[buf_tok].reshape(n_blocks, MOE_BLOCK, d)
    block_e = jnp.minimum(jnp.searchsorted(padded_ends, jnp.arange(n_blocks) * MOE_BLOCK, side='right'),
                          e_count - 1)

    def block_ffn(args):
        xblk, e = args
        hid = jax.nn.silu(xblk @ w_gate[e]) * (xblk @ w_up[e])
        return hid @ w_down[e]

    yb = lax.map(block_ffn, (xb, block_e)).reshape(rows, d)
    y = jnp.zeros((n_assign, d), yb.dtype).at[order].set(yb[dest])
    return y.reshape(n, MOE_TOP_K, d)


def hierarchical_moe(h, rg_w, rg_b, re_w, re_b, w_gate, w_up, w_down):
    f32 = jnp.float32
    bsz, t_len, d = h.shape
    tokens = h.reshape(-1, d)
    n = tokens.shape[0]
    g_prob = jax.nn.softmax((tokens @ rg_w).astype(f32) + rg_b.astype(f32), axis=-1)
    g_p, g_idx = lax.top_k(g_prob, 1)
    e_logits = ((tokens @ re_w).astype(f32) + re_b.astype(f32)).reshape(n, MOE_GROUPS, MOE_EXPERTS_PER_GROUP)
    e_logits = jnp.take_along_axis(e_logits, g_idx[:, :, None], axis=1)[:, 0]
    e_p, e_local = lax.top_k(jax.nn.softmax(e_logits, axis=-1), MOE_TOP_K)
    e_p = e_p / jnp.sum(e_p, axis=-1, keepdims=True)
    weights = g_p * e_p
    expert_ids = g_idx * MOE_EXPERTS_PER_GROUP + e_local
    y = grouped_expert_ffn(tokens, expert_ids, w_gate, w_up, w_down)
    out = jnp.einsum('nk,nkd->nd', weights.astype(y.dtype), y)
    return out.reshape(bsz, t_len, d)


def setup_inputs(seed: int = 0) -> dict:
    key = jax.random.key(seed)
    ks = jax.random.split(key, 32)
    f32 = jnp.float32
    L, D = DEPTH, D_MODEL
    res_scale = (2.0 * DEPTH) ** -0.5

    def nrm(k, shape, scale):
        return jax.random.normal(k, shape, f32) * scale

    dt0 = jnp.exp(jax.random.uniform(ks[10], (L, SSM_HEADS), f32, math.log(1e-3), math.log(1e-1)))
    return {
        "x": jax.random.normal(ks[0], (BATCH, SEQ, D), f32),
        "norm1_w": 1.0 + nrm(ks[1], (L, D), 0.02),
        "w_in": nrm(ks[2], (L, D, IN_PROJ_WIDTH), D ** -0.5),
        "gla_gk_up": nrm(ks[3], (L, GLA_GATE_RANK, GLA_KW), GLA_GATE_RANK ** -0.5),
        "gla_gk_bias": nrm(ks[4], (L, GLA_KW), 0.1),
        "gla_norm_w": 1.0 + nrm(ks[5], (L, GLA_VW), 0.02),
        "hgrn_lb_logits": nrm(ks[6], (L, HGRN_W), 0.5),
        "hgrn_norm_w": 1.0 + nrm(ks[7], (L, HGRN_W), 0.02),
        "ssm_conv_w": nrm(ks[8], (L, SSM_CONV, SSM_CONV_DIM), SSM_CONV ** -0.5),
        "ssm_conv_b": nrm(ks[9], (L, SSM_CONV_DIM), 0.02),
        "ssm_dt_bias": dt0 + jnp.log(-jnp.expm1(-dt0)),
        "ssm_a_log": jnp.log(jax.random.uniform(ks[11], (L, SSM_HEADS), f32, 1.0, 16.0)),
        "ssm_d": 1.0 + nrm(ks[12], (L, SSM_HEADS), 0.02),
        "ssm_norm_w": 1.0 + nrm(ks[13], (L, SSM_D_INNER), 0.02),
        "w_br_gla": nrm(ks[14], (L, GLA_VW, D), GLA_VW ** -0.5),
        "w_br_hgrn": nrm(ks[15], (L, HGRN_W, D), HGRN_W ** -0.5),
        "w_br_ssm": nrm(ks[16], (L, SSM_D_INNER, D), SSM_D_INNER ** -0.5),
        "w_out": nrm(ks[17], (L, D, D), D ** -0.5 * res_scale),
        "norm2_w": 1.0 + nrm(ks[18], (L, D), 0.02),
        "router_group_w": nrm(ks[19], (L, D, MOE_GROUPS), D ** -0.5),
        "router_group_b": nrm(ks[20], (L, MOE_GROUPS), 0.01),
        "router_expert_w": nrm(ks[21], (L, D, MOE_EXPERTS), D ** -0.5),
        "router_expert_b": nrm(ks[22], (L, MOE_EXPERTS), 0.01),
        "moe_w_gate": nrm(ks[23], (L, MOE_EXPERTS, D, MOE_FF), D ** -0.5),
        "moe_w_up": nrm(ks[24], (L, MOE_EXPERTS, D, MOE_FF), D ** -0.5),
        "moe_w_down": nrm(ks[25], (L, MOE_EXPERTS, MOE_FF, D), MOE_FF ** -0.5 * res_scale),
        "final_norm_w": 1.0 + nrm(ks[26], (D,), 0.02),
    }


def reference(x, norm1_w, w_in, gla_gk_up, gla_gk_bias, gla_norm_w, hgrn_lb_logits, hgrn_norm_w,
              ssm_conv_w, ssm_conv_b, ssm_dt_bias, ssm_a_log, ssm_d, ssm_norm_w,
              w_br_gla, w_br_hgrn, w_br_ssm, w_out, norm2_w,
              router_group_w, router_group_b, router_expert_w, router_expert_b,
              moe_w_gate, moe_w_up, moe_w_down, final_norm_w):
    lower_bounds = jnp.cumsum(jax.nn.softmax(hgrn_lb_logits.astype(jnp.float32), axis=0), axis=0)
    for l in range(DEPTH):
        lb = lower_bounds[l] - lower_bounds[0]
        h = rms_norm(x, norm1_w[l])
        x = x + hybrid_mixer(h, w_in[l], gla_gk_up[l], gla_gk_bias[l], gla_norm_w[l], lb, hgrn_norm_w[l],
                             ssm_conv_w[l], ssm_conv_b[l], ssm_dt_bias[l], ssm_a_log[l], ssm_d[l],
                             ssm_norm_w[l], w_br_gla[l], w_br_hgrn[l], w_br_ssm[l], w_out[l])
        h = rms_norm(x, norm2_w[l])
        x = x + hierarchical_moe(h, router_group_w[l], router_group_b[l], router_expert_w[l],
                                 router_expert_b[l], moe_w_gate[l], moe_w_up[l], moe_w_down[l])
    return rms_norm(x, final_norm_w)
```

```python
import functools

import jax
import jax.numpy as jnp
from jax import lax
from jax.experimental import pallas as pl
from jax.experimental.pallas import tpu as pltpu

F32 = jnp.float32
BF16 = jnp.bfloat16

D_MODEL = 1024
DEPTH = 4
NORM_EPS = 1e-6
HEADS = 4
HEAD_W = 128
MIX_W = HEADS * HEAD_W
GLA_DK = 64
GLA_RANK = 16
GLA_TEMP = 16.0
SSM_HEADS = 16
SSM_P = 64
SSM_GROUPS = 4
SSM_N = 128
SSM_CONV = 4
N_EXPERTS = 32
EXPERTS_PER_GROUP = 8
MOE_GROUPS = 4
MOE_FF = 512

LANES = 128
SUBLANES = 8

LA_CHUNK = 64
LA_SUB = 16
SSD_CHUNK = 128

BF_COLS = dict(sxbc=(0, 2048), sz=(2048, 1024), gates=(3072, 3072), gv=(6144, 512), gr=(6656, 512),
               hq=(7168, 512), hi=(7680, 512), hg=(8192, 512), gq=(8704, 512), gk=(9216, 512))
BF_WIDTH = 9728
F_COLS = dict(hf=(0, 512), glow=(512, 128), sdt=(640, 128))
F_WIDTH = 768

VMEM_LIMIT = 56 * 1024 * 1024


def _cparams(sem):
    return pltpu.CompilerParams(dimension_semantics=sem, vmem_limit_bytes=VMEM_LIMIT)


def _log_sigmoid(x):
    return jnp.minimum(x, 0.0) - jnp.log1p(jnp.exp(-jnp.abs(x)))


def _softplus(x):
    return jnp.maximum(x, 0.0) + jnp.log1p(jnp.exp(-jnp.abs(x)))


def _silu(x):
    return x * jax.nn.sigmoid(x)


def _dot(a, b):
    return jnp.dot(a, b, preferred_element_type=F32)


def _dot_nt(a, b):
    return lax.dot_general(a, b, (((1,), (1,)), ((), ())), preferred_element_type=F32)


def _dot_tn(a, b):
    return lax.dot_general(a, b, (((0,), (0,)), ((), ())), preferred_element_type=F32)


def _split3(x):
    x1 = x.astype(BF16)
    r1 = x - x1.astype(F32)
    x2 = r1.astype(BF16)
    r2 = r1 - x2.astype(F32)
    return x1, x2, r2.astype(BF16)


def _cumsum_rows(tril, x):
    x1, x2, x3 = _split3(x)
    return _dot(tril, x1) + _dot(tril, x2) + _dot(tril, x3)


def _tril_ones(n):
    r = lax.broadcasted_iota(jnp.int32, (n, n), 0)
    c = lax.broadcasted_iota(jnp.int32, (n, n), 1)
    return r >= c


def _inproj_kernel(x_ref, nw_ref, wb_ref, wf_ref, ob_ref, of_ref):
    x = x_ref[...]
    h = x * lax.rsqrt(jnp.mean(x * x, axis=-1, keepdims=True) + NORM_EPS) * nw_ref[...]
    h = h.astype(BF16)
    step = 512
    for c in range(0, BF_WIDTH, step):
        ob_ref[:, c:c + step] = _dot(h, wb_ref[:, c:c + step]).astype(ob_ref.dtype)
    of_ref[...] = _dot(h, wf_ref[...])


def _inproj(x2d, nw, wb, wf, layer, tm=256):
    n = x2d.shape[0]
    return pl.pallas_call(
        _inproj_kernel,
        grid=(n // tm,),
        in_specs=[
            pl.BlockSpec((tm, D_MODEL), lambda i: (i, 0)),
            pl.BlockSpec((None, 1, D_MODEL), lambda i: (layer, 0, 0)),
            pl.BlockSpec((None, D_MODEL, BF_WIDTH), lambda i: (layer, 0, 0), pipeline_mode=pl.Buffered(1)),
            pl.BlockSpec((None, D_MODEL, F_WIDTH), lambda i: (layer, 0, 0), pipeline_mode=pl.Buffered(1)),
        ],
        out_specs=[
            pl.BlockSpec((tm, BF_WIDTH), lambda i: (i, 0)),
            pl.BlockSpec((tm, F_WIDTH), lambda i: (i, 0)),
        ],
        out_shape=[jax.ShapeDtypeStruct((n, BF_WIDTH), BF16), jax.ShapeDtypeStruct((n, F_WIDTH), F32)],
        compiler_params=_cparams(("parallel",)),
        name="inproj",
    )(x2d, nw, wb, wf)


def _la_chunk_head(qh, kh, vh, bh, hsl, kc_ref, bc_ref, st_ref, h):
    C = LA_CHUNK
    row8 = lax.broadcasted_iota(jnp.int32, (SUBLANES, 1), 0)
    colc = lax.broadcasted_iota(jnp.int32, (SUBLANES, C), 1)
    b_last = bh[C - 1:C, :]
    st = st_ref[h]
    o = _dot_nt((qh * jnp.exp(bh)).astype(BF16), st.astype(BF16))
    tiles = []
    for blk in range(C // LA_SUB):
        r0 = blk * LA_SUB
        if blk == 0:
            halves = [jnp.zeros((SUBLANES, C), F32), jnp.zeros((SUBLANES, C), F32)]
        else:
            bs = bh[r0:r0 + 1, :]
            qi = (qh[r0:r0 + LA_SUB] * jnp.exp(bh[r0:r0 + LA_SUB] - bs)).astype(BF16)
            kj = kh[0:r0] * jnp.exp(bs - bh[0:r0])
            kj = jnp.concatenate([kj, jnp.zeros((C - r0, HEAD_W), F32)], axis=0).astype(BF16)
            s = _dot_nt(qi, kj)
            halves = [s[0:SUBLANES], s[SUBLANES:LA_SUB]]
        for j in range(LA_SUB):
            bj = bc_ref[pl.ds(r0 + j, 1), hsl]
            kj = kc_ref[pl.ds(r0 + j, 1), hsl]
            for half in range(2):
                if j >= SUBLANES * (half + 1):
                    continue
                rs = slice(r0 + SUBLANES * half, r0 + SUBLANES * (half + 1))
                d = bh[rs] - bj
                p = qh[rs] * kj
                if j >= SUBLANES * half:
                    m = row8 >= (j - SUBLANES * half)
                    p = jnp.where(m, p * jnp.exp(jnp.where(m, d, 0.0)), 0.0)
                else:
                    p = p * jnp.exp(d)
                r = jnp.sum(p, axis=-1, keepdims=True)
                halves[half] = jnp.where(colc == (r0 + j), r, halves[half])
        tiles += halves
    a = jnp.concatenate(tiles, axis=0).astype(BF16)
    o = o + _dot(a, vh)
    kte = (kh * jnp.exp(b_last - bh)).astype(BF16)
    st_ref[h] = st * jnp.exp(b_last) + _dot_tn(vh, kte)
    return o


def _la_kernel(mode, nchunks, *refs):
    if mode == "gla":
        (q_ref, k_ref, v_ref, gate_ref, gl_ref, up_ref, gb_ref, nw_ref,
         o_ref, st_ref, kc_ref, bc_ref) = refs
    else:
        (q_ref, v_ref, gate_ref, f_ref, loglb_ref, log1m_ref, onem_ref, nw_ref,
         o_ref, st_ref, kc_ref, bc_ref) = refs

    @pl.when(pl.program_id(1) == 0)
    def _():
        st_ref[...] = jnp.zeros_like(st_ref)

    tril = _tril_ones(LA_CHUNK).astype(BF16)

    def body(c, carry):
        rs = pl.ds(pl.multiple_of(c * LA_CHUNK, LA_CHUNK), LA_CHUNK)
        if mode == "gla":
            q = q_ref[rs, :].astype(F32) * (GLA_DK ** -0.5)
            k = k_ref[rs, :].astype(F32)
            logit = _dot(gl_ref[rs, :].astype(BF16), up_ref[...]) + gb_ref[...]
            g = _log_sigmoid(logit) * (1.0 / GLA_TEMP)
        else:
            q = _silu(q_ref[rs, :].astype(F32)) * (HEAD_W ** -0.5)
            f = f_ref[rs, :]
            u = loglb_ref[...]
            w = log1m_ref[...] + _log_sigmoid(f)
            g = jnp.maximum(u, w) + jnp.log1p(jnp.exp(-jnp.abs(u - w)))
            k = onem_ref[...] * jax.nn.sigmoid(-f)
        v = v_ref[rs, :]
        b = _cumsum_rows(tril, g)
        kc_ref[...] = k
        bc_ref[...] = b
        for h in range(HEADS):
            hsl = slice(h * HEAD_W, (h + 1) * HEAD_W)
            o = _la_chunk_head(q[:, hsl], k[:, hsl], v[:, hsl], b[:, hsl], hsl, kc_ref, bc_ref, st_ref, h)
            var = jnp.mean(o * o, axis=-1, keepdims=True)
            y = o * lax.rsqrt(var + NORM_EPS) * nw_ref[:, hsl]
            y = y * _silu(gate_ref[rs, hsl].astype(F32))
            o_ref[rs, hsl] = y.astype(o_ref.dtype)
        return carry

    lax.fori_loop(0, nchunks, body, 0)


def _col_spec(tb, tblocks, width, off):
    assert off % width == 0
    cb = off // width
    return pl.BlockSpec((tb, width), lambda b, t: (b * tblocks + t, cb))


def _row_param(width, layer):
    return pl.BlockSpec((None, 1, width), lambda b, t: (layer, 0, 0))


def _lin_attn(mode, pb, pf, params, layer, bsz, t_len, tb=512):
    n = bsz * t_len
    tblocks = t_len // tb
    sp = functools.partial(_col_spec, tb, tblocks)
    if mode == "gla":
        up, gb, nw = params
        ins = [pb, pb, pb, pb, pf, up, gb, nw]
        specs = [sp(MIX_W, BF_COLS["gq"][0]), sp(MIX_W, BF_COLS["gk"][0]), sp(MIX_W, BF_COLS["gv"][0]),
                 sp(MIX_W, BF_COLS["gr"][0]), sp(LANES, F_COLS["glow"][0]),
                 pl.BlockSpec((None, LANES, MIX_W), lambda b, t: (layer, 0, 0)),
                 _row_param(MIX_W, layer), _row_param(MIX_W, layer)]
    else:
        loglb, log1m, onem, nw = params
        ins = [pb, pb, pb, pf, loglb, log1m, onem, nw]
        specs = [sp(MIX_W, BF_COLS["hq"][0]), sp(MIX_W, BF_COLS["hi"][0]), sp(MIX_W, BF_COLS["hg"][0]),
                 sp(MIX_W, F_COLS["hf"][0]),
                 _row_param(MIX_W, layer), _row_param(MIX_W, layer), _row_param(MIX_W, layer),
                 _row_param(MIX_W, layer)]
    return pl.pallas_call(
        functools.partial(_la_kernel, mode, tb // LA_CHUNK),
        grid=(bsz, tblocks),
        in_specs=specs,
        out_specs=pl.BlockSpec((tb, MIX_W), lambda b, t: (b * tblocks + t, 0)),
        out_shape=jax.ShapeDtypeStruct((n, MIX_W), BF16),
        scratch_shapes=[pltpu.VMEM((HEADS, HEAD_W, HEAD_W), F32),
                        pltpu.VMEM((LA_CHUNK, MIX_W), F32),
                        pltpu.VMEM((LA_CHUNK, MIX_W), F32)],
        compiler_params=_cparams(("parallel", "arbitrary")),
        name="lin_attn_" + mode,
    )(*ins)


def _ssd_kernel(nchunks, tb, xbc_ref, z_ref, dt_ref, cw_ref, cb_ref, dtb_ref, alog_ref, dvec_ref, nw_ref,
                o_ref, ubuf, xc_ref, st_ref):
    t = pl.program_id(1)

    @pl.when(t == 0)
    def _():
        ubuf[0:SUBLANES, :] = jnp.zeros((SUBLANES, ubuf.shape[1]), F32)
        st_ref[...] = jnp.zeros_like(st_ref)

    @pl.when(t > 0)
    def _():
        ubuf[0:SUBLANES, :] = ubuf[tb:tb + SUBLANES, :]

    ubuf[SUBLANES:SUBLANES + tb, :] = xbc_ref[...].astype(F32)
    acc = cb_ref[...] + ubuf[pl.ds(SUBLANES - SSM_CONV + 1, tb), :] * cw_ref[0:1, :]
    for w in range(1, SSM_CONV):
        acc = acc + ubuf[pl.ds(SUBLANES - SSM_CONV + 1 + w, tb), :] * cw_ref[w:w + 1, :]
    xc_ref[...] = _silu(acc)

    C = SSD_CHUNK
    causal = _tril_ones(C)
    tril = causal.astype(BF16)
    lane = lax.broadcasted_iota(jnp.int32, (C, LANES), 1)
    row = lax.broadcasted_iota(jnp.int32, (C, LANES), 0)
    lo_lane = lane < SSM_P
    lo_row = row < SSM_P
    a_neg = -jnp.exp(alog_ref[...])
    d_inner = SSM_HEADS * SSM_P
    gn = SSM_GROUPS * SSM_N
    hpg = SSM_HEADS // SSM_GROUPS

    def body(c, carry):
        rs = pl.ds(pl.multiple_of(c * C, C), C)
        dt = _softplus(dt_ref[rs, :] + dtb_ref[...])
        b_col = _cumsum_rows(tril, dt * a_neg)
        b_row = b_col.T
        dt_row = dt.T
        b_last = b_col[C - 1:C, :]
        e_col = jnp.exp(b_col)
        wts = jnp.exp(b_last - b_col) * dt
        e_last = jnp.exp(b_last)
        for g in range(SSM_GROUPS):
            bg = xc_ref[rs, d_inner + g * SSM_N:d_inner + (g + 1) * SSM_N].astype(BF16)
            cg = xc_ref[rs, d_inner + gn + g * SSM_N:d_inner + gn + (g + 1) * SSM_N].astype(BF16)
            cbm = _dot_nt(cg, bg)
            ys = []
            for pp in range(hpg // 2):
                p = g * (hpg // 2) + pp
                ha, hb = 2 * p, 2 * p + 1
                psl = slice(p * LANES, (p + 1) * LANES)
                x2 = xc_ref[rs, psl]
                x2b = x2.astype(BF16)

                def wmat(h):
                    seg = b_col[:, h:h + 1] - b_row[h:h + 1, :]
                    dec = jnp.where(causal, jnp.exp(jnp.where(causal, seg, 0.0)), 0.0)
                    return (dec * cbm * dt_row[h:h + 1, :]).astype(BF16)

                y2 = jnp.where(lo_lane, _dot(wmat(ha), x2b), _dot(wmat(hb), x2b))
                stp = st_ref[p]
                sc = jnp.where(lo_lane, e_col[:, ha:ha + 1], e_col[:, hb:hb + 1])
                y2 = y2 + _dot_nt(cg, stp.astype(BF16)) * sc
                wsel = jnp.where(lo_lane, wts[:, ha:ha + 1], wts[:, hb:hb + 1])
                upd = _dot_tn((x2 * wsel).astype(BF16), bg)
                dsel = jnp.where(lo_row, e_last[:, ha:ha + 1], e_last[:, hb:hb + 1])
                st_ref[p] = stp * dsel + upd
                y2 = y2 + dvec_ref[:, psl] * x2
                y2 = y2 * _silu(z_ref[rs, psl].astype(F32))
                ys.append(y2)
            yg = jnp.concatenate(ys, axis=-1)
            var = jnp.mean(yg * yg, axis=-1, keepdims=True)
            gsl = slice(g * 2 * LANES, (g + 1) * 2 * LANES)
            o_ref[rs, gsl] = (yg * lax.rsqrt(var + NORM_EPS) * nw_ref[:, gsl]).astype(o_ref.dtype)
        return carry

    lax.fori_loop(0, nchunks, body, 0)


def _ssd(pb, pf, cw, cb, dtb, alog, dvec, nw, layer, bsz, t_len, tb=256):
    n = bsz * t_len
    tblocks = t_len // tb
    sp = functools.partial(_col_spec, tb, tblocks)
    conv_dim = SSM_HEADS * SSM_P + 2 * SSM_GROUPS * SSM_N
    d_inner = SSM_HEADS * SSM_P
    return pl.pallas_call(
        functools.partial(_ssd_kernel, tb // SSD_CHUNK, tb),
        grid=(bsz, tblocks),
        in_specs=[sp(conv_dim, BF_COLS["sxbc"][0]), sp(d_inner, BF_COLS["sz"][0]), sp(LANES, F_COLS["sdt"][0]),
                  pl.BlockSpec((None, SSM_CONV, conv_dim), lambda b, t: (layer, 0, 0)),
                  _row_param(conv_dim, layer), _row_param(LANES, layer), _row_param(LANES, layer),
                  _row_param(d_inner, layer), _row_param(d_inner, layer)],
        out_specs=pl.BlockSpec((tb, d_inner), lambda b, t: (b * tblocks + t, 0)),
        out_shape=jax.ShapeDtypeStruct((n, d_inner), BF16),
        scratch_shapes=[pltpu.VMEM((tb + 2 * SUBLANES, conv_dim), F32),
                        pltpu.VMEM((tb, conv_dim), F32),
                        pltpu.VMEM((SSM_HEADS // 2, 2 * SSM_P, SSM_N), F32)],
        compiler_params=_cparams(("parallel", "arbitrary")),
        name="ssd",
    )(pb, pb, pf, cw, cb, dtb, alog, dvec, nw)


def _merge_kernel(yg_ref, yh_ref, ys_ref, gates_ref, x_ref, w1_ref, w2_ref, w3_ref, wo_ref,
                  n2_ref, rwh_ref, rwl_ref, rb_ref, xo_ref, h2_ref, ri_ref):
    g = jax.nn.sigmoid(gates_ref[...].astype(F32))
    m = (g[:, 0:D_MODEL] * _dot(yg_ref[...], w1_ref[...])
         + g[:, D_MODEL:2 * D_MODEL] * _dot(yh_ref[...], w2_ref[...])
         + g[:, 2 * D_MODEL:3 * D_MODEL] * _dot(ys_ref[...], w3_ref[...]))
    xn = x_ref[...] + _dot(m.astype(BF16), wo_ref[...])
    xo_ref[...] = xn
    h2 = xn * lax.rsqrt(jnp.mean(xn * xn, axis=-1, keepdims=True) + NORM_EPS) * n2_ref[...]
    h2_ref[...] = h2
    hi = h2.astype(BF16)
    lo = (h2 - hi.astype(F32)).astype(BF16)
    logits = _dot(hi, rwh_ref[...]) + _dot(lo, rwh_ref[...]) + _dot(hi, rwl_ref[...]) + rb_ref[...]

    lane = lax.broadcasted_iota(jnp.int32, logits.shape, 1)
    lanef = lane.astype(F32)
    neg = jnp.float32(-1e30)
    big = jnp.float32(1e9)
    gmask = lane < MOE_GROUPS
    gl = jnp.where(gmask, logits, neg)
    gexp = jnp.where(gmask, jnp.exp(gl - jnp.max(gl, axis=-1, keepdims=True)), 0.0)
    gprob = gexp / jnp.sum(gexp, axis=-1, keepdims=True)
    g_p = jnp.max(gprob, axis=-1, keepdims=True)
    g_idx = jnp.min(jnp.where(gmask & (gprob == g_p), lanef, big), axis=-1, keepdims=True)
    lo_l = MOE_GROUPS + EXPERTS_PER_GROUP * g_idx
    emask = (lanef >= lo_l) & (lanef < lo_l + EXPERTS_PER_GROUP)
    el = jnp.where(emask, logits, neg)
    eexp = jnp.where(emask, jnp.exp(el - jnp.max(el, axis=-1, keepdims=True)), 0.0)
    eprob = eexp / jnp.sum(eexp, axis=-1, keepdims=True)
    p1 = jnp.max(jnp.where(emask, eprob, -1.0), axis=-1, keepdims=True)
    i1 = jnp.min(jnp.where(emask & (eprob == p1), lanef, big), axis=-1, keepdims=True)
    rest = emask & (lanef != i1)
    p2 = jnp.max(jnp.where(rest, eprob, -1.0), axis=-1, keepdims=True)
    i2 = jnp.min(jnp.where(rest & (eprob == p2), lanef, big), axis=-1, keepdims=True)
    den = p1 + p2
    w_a = g_p * (p1 / den)
    w_b = g_p * (p2 / den)
    ri_ref[...] = jnp.where(lane == 0, i1 - MOE_GROUPS,
                            jnp.where(lane == 1, i2 - MOE_GROUPS,
                                      jnp.where(lane == 2, w_a, jnp.where(lane == 3, w_b, 0.0))))


def _merge(yg, yh, ys, pb, x2d, w1, w2, w3, wo, n2, rwh, rwl, rb, layer, tm=256):
    n = x2d.shape[0]
    gates_blk = BF_COLS["gates"][0] // BF_COLS["gates"][1]

    def wspec(k):
        return pl.BlockSpec((None, k, D_MODEL), lambda i: (layer, 0, 0))

    return pl.pallas_call(
        _merge_kernel,
        grid=(n // tm,),
        in_specs=[pl.BlockSpec((tm, MIX_W), lambda i: (i, 0)),
                  pl.BlockSpec((tm, MIX_W), lambda i: (i, 0)),
                  pl.BlockSpec((tm, D_MODEL), lambda i: (i, 0)),
                  pl.BlockSpec((tm, 3 * D_MODEL), lambda i: (i, gates_blk)),
                  pl.BlockSpec((tm, D_MODEL), lambda i: (i, 0)),
                  wspec(MIX_W), wspec(MIX_W), wspec(D_MODEL), wspec(D_MODEL),
                  pl.BlockSpec((None, 1, D_MODEL), lambda i: (layer, 0, 0)),
                  pl.BlockSpec((None, D_MODEL, LANES), lambda i: (layer, 0, 0)),
                  pl.BlockSpec((None, D_MODEL, LANES), lambda i: (layer, 0, 0)),
                  pl.BlockSpec((None, 1, LANES), lambda i: (layer, 0, 0))],
        out_specs=[pl.BlockSpec((tm, D_MODEL), lambda i: (i, 0)),
                   pl.BlockSpec((tm, D_MODEL), lambda i: (i, 0)),
                   pl.BlockSpec((tm, LANES), lambda i: (i, 0))],
        out_shape=[jax.ShapeDtypeStruct((n, D_MODEL), F32),
                   jax.ShapeDtypeStruct((n, D_MODEL), F32),
                   jax.ShapeDtypeStruct((n, LANES), F32)],
        compiler_params=_cparams(("parallel",)),
        name="merge_router",
    )(yg, yh, ys, pb, x2d, w1, w2, w3, wo, n2, rwh, rwl, rb)


def _dispatch_kernel(tp, nsteps, idx_hbm, h_ref, xb_in, xb_out, idx_smem, isem, rsem):
    del xb_in
    i = pl.program_id(0)
    slot = i % 2

    def idx_copy(step, s):
        return pltpu.make_async_copy(idx_hbm.at[step], idx_smem.at[s], isem.at[s])

    def row_copy(r, dst):
        return pltpu.make_async_copy(h_ref.at[pl.ds(r, 1)], xb_out.at[pl.ds(dst, 1)], rsem)

    @pl.when(i == 0)
    def _():
        idx_copy(0, 0).start()
        if nsteps > 1:
            idx_copy(1, 1).start()

    idx_copy(i, slot).wait()

    def issue(r, carry):
        for k in range(2):
            row_copy(r, idx_smem[slot, k * tp + r]).start()
        return carry

    lax.fori_loop(0, tp, issue, 0, unroll=8)

    @pl.when(i + 2 < nsteps)
    def _():
        idx_copy(i + 2, slot).start()

    def drain(r, carry):
        for k in range(2):
            row_copy(0, 0).wait()
        return carry

    lax.fori_loop(0, tp, drain, 0, unroll=8)


def _dispatch(dest_tiles, h2, xb_init, tp):
    n = h2.shape[0]
    rows = xb_init.shape[0]
    return pl.pallas_call(
        functools.partial(_dispatch_kernel, tp, n // tp),
        grid=(n // tp,),
        in_specs=[pl.BlockSpec(memory_space=pl.ANY),
                  pl.BlockSpec((tp, D_MODEL), lambda i: (i, 0)),
                  pl.BlockSpec(memory_space=pl.ANY)],
        out_specs=pl.BlockSpec(memory_space=pl.ANY),
        out_shape=jax.ShapeDtypeStruct((rows, D_MODEL), F32),
        scratch_shapes=[pltpu.SMEM((2, 2 * tp), jnp.int32),
                        pltpu.SemaphoreType.DMA((2,)),
                        pltpu.SemaphoreType.DMA(())],
        input_output_aliases={2: 0},
        compiler_params=pltpu.CompilerParams(dimension_semantics=("arbitrary",), vmem_limit_bytes=VMEM_LIMIT,
                                             has_side_effects=True),
        name="moe_dispatch",
    )(dest_tiles, h2, xb_init)


def _ffn_kernel(be_ref, nb_ref, x_ref, wg_ref, wu_ref, wd_ref, o_ref):
    i = pl.program_id(0)

    @pl.when(i < nb_ref[0])
    def _():
        x = x_ref[...].astype(BF16)
        hid = _silu(_dot(x, wg_ref[...])) * _dot(x, wu_ref[...])
        o_ref[...] = _dot(hid.astype(BF16), wd_ref[...])

    @pl.when(i >= nb_ref[0])
    def _():
        o_ref[...] = jnp.zeros_like(o_ref)


def _ffn(block_e, nblk_used, xb, wg, wu, wd, layer, blk):
    rows = xb.shape[0]
    grid_spec = pltpu.PrefetchScalarGridSpec(
        num_scalar_prefetch=2,
        grid=(rows // blk,),
        in_specs=[pl.BlockSpec((blk, D_MODEL), lambda i, be, nb: (i, 0)),
                  pl.BlockSpec((None, None, D_MODEL, MOE_FF), lambda i, be, nb: (layer, be[i], 0, 0)),
                  pl.BlockSpec((None, None, D_MODEL, MOE_FF), lambda i, be, nb: (layer, be[i], 0, 0)),
                  pl.BlockSpec((None, None, MOE_FF, D_MODEL), lambda i, be, nb: (layer, be[i], 0, 0))],
        out_specs=pl.BlockSpec((blk, D_MODEL), lambda i, be, nb: (i, 0)),
    )
    return pl.pallas_call(
        _ffn_kernel,
        grid_spec=grid_spec,
        out_shape=jax.ShapeDtypeStruct((rows, D_MODEL), F32),
        compiler_params=_cparams(("arbitrary",)),
        name="moe_ffn",
    )(block_e, nblk_used, xb, wg, wu, wd)


def _combine_kernel(tc, nsteps, final, idx_hbm, yb_hbm, x_ref, w_ref, fw_ref, o_ref, idx_smem, ybuf, isem, rsem):
    i = pl.program_id(0)
    slot = i % 2

    def idx_copy(step, s):
        return pltpu.make_async_copy(idx_hbm.at[step], idx_smem.at[s], isem.at[s])

    def row_copy(src, r, s):
        return pltpu.make_async_copy(yb_hbm.at[pl.ds(src, 1)], ybuf.at[s, pl.ds(r, 1)], rsem.at[s])

    def issue_rows(s):
        def issue(r, carry):
            row_copy(idx_smem[s, r], r, s).start()
            return carry
        lax.fori_loop(0, 2 * tc, issue, 0, unroll=8)

    @pl.when(i == 0)
    def _():
        idx_copy(0, 0).start()
        idx_copy(0, 0).wait()
        issue_rows(0)
        if nsteps > 1:
            idx_copy(1, 1).start()

    @pl.when(i + 1 < nsteps)
    def _():
        idx_copy(i + 1, 1 - slot).wait()
        issue_rows(1 - slot)

    @pl.when(i + 2 < nsteps)
    def _():
        idx_copy(i + 2, slot).start()

    def drain(r, carry):
        row_copy(0, 0, slot).wait()
        return carry

    lax.fori_loop(0, 2 * tc, drain, 0, unroll=8)
    w = w_ref[...]
    out = x_ref[...] + w[:, 2:3] * ybuf[slot, 0:tc, :] + w[:, 3:4] * ybuf[slot, tc:2 * tc, :]
    if final:
        out = out * lax.rsqrt(jnp.mean(out * out, axis=-1, keepdims=True) + NORM_EPS) * fw_ref[...]
    o_ref[...] = out


def _combine(dest_tiles, yb, x2d, ri, fw, final, tc):
    n = x2d.shape[0]
    return pl.pallas_call(
        functools.partial(_combine_kernel, tc, n // tc, final),
        grid=(n // tc,),
        in_specs=[pl.BlockSpec(memory_space=pl.ANY),
                  pl.BlockSpec(memory_space=pl.ANY),
                  pl.BlockSpec((tc, D_MODEL), lambda i: (i, 0)),
                  pl.BlockSpec((tc, LANES), lambda i: (i, 0)),
                  pl.BlockSpec((1, D_MODEL), lambda i: (0, 0))],
        out_specs=pl.BlockSpec((tc, D_MODEL), lambda i: (i, 0)),
        out_shape=jax.ShapeDtypeStruct((n, D_MODEL), F32),
        scratch_shapes=[pltpu.SMEM((2, 2 * tc), jnp.int32),
                        pltpu.VMEM((2, 2 * tc, D_MODEL), F32),
                        pltpu.SemaphoreType.DMA((2,)),
                        pltpu.SemaphoreType.DMA((2,))],
        compiler_params=_cparams(("arbitrary",)),
        name="moe_combine",
    )(dest_tiles, yb, x2d, ri, fw)


def _route_slots(ri, blk):
    n = ri.shape[0]
    e = ri[:, 0:2].astype(jnp.int32)
    flat_e = e.reshape(-1)
    onehot = (flat_e[:, None] == jnp.arange(N_EXPERTS, dtype=jnp.int32)[None, :]).astype(jnp.int32)
    csum = jnp.cumsum(onehot, axis=0)
    pos = jnp.sum(csum * onehot, axis=1) - 1
    counts = csum[-1]
    padded = (counts + blk - 1) // blk * blk
    pend = jnp.cumsum(padded)
    pstart = pend - padded
    dest = (jnp.sum(onehot * pstart[None, :], axis=1) + pos).astype(jnp.int32).reshape(n, 2)
    rows = 2 * n + N_EXPERTS * blk
    nblk = rows // blk
    block_start = jnp.arange(nblk, dtype=jnp.int32) * blk
    block_e = jnp.minimum(jnp.sum((block_start[:, None] >= pend[None, :]).astype(jnp.int32), axis=1),
                          N_EXPERTS - 1).astype(jnp.int32)
    nblk_used = (pend[-1] // blk).astype(jnp.int32).reshape(1)
    return dest, block_e, nblk_used, rows


def _tile_slots(dest, tile):
    n = dest.shape[0]
    return dest.reshape(n // tile, tile, 2).transpose(0, 2, 1).reshape(n // tile, 2 * tile)


def _pad_heads(w, heads, width):
    lead = w.shape[:-1]
    w = w.reshape(lead + (heads, width))
    w = jnp.pad(w, [(0, 0)] * len(lead) + [(0, 0), (0, HEAD_W - width)])
    return w.reshape(lead + (heads * HEAD_W,))


def _pad_last(w, width):
    return jnp.pad(w, [(0, 0)] * (w.ndim - 1) + [(0, width - w.shape[-1])])


def _prep_inproj(w_in):
    gla_kw, gla_vw, hw = HEADS * GLA_DK, MIX_W, MIX_W
    d_inner = SSM_HEADS * SSM_P
    conv_dim = d_inner + 2 * SSM_GROUPS * SSM_N
    sizes = (gla_kw, gla_kw, gla_vw, GLA_RANK, gla_vw, hw, hw, hw, hw, d_inner, conv_dim, SSM_HEADS, 3 * D_MODEL)
    offs = [0]
    for s in sizes:
        offs.append(offs[-1] + s)
    seg = {name: w_in[..., offs[i]:offs[i + 1]] for i, name in enumerate(
        ("gq", "gk", "gv", "glow", "gr", "hq", "hf", "hi", "hg", "sz", "sxbc", "sdt", "gates"))}
    seg["gq"] = _pad_heads(seg["gq"], HEADS, GLA_DK)
    seg["gk"] = _pad_heads(seg["gk"], HEADS, GLA_DK)
    seg["glow"] = _pad_last(seg["glow"], LANES)
    seg["sdt"] = _pad_last(seg["sdt"], LANES)
    wb = jnp.concatenate([seg[k] for k in sorted(BF_COLS, key=lambda k: BF_COLS[k][0])], axis=-1).astype(BF16)
    wf = jnp.concatenate([seg[k] for k in sorted(F_COLS, key=lambda k: F_COLS[k][0])], axis=-1).astype(BF16)
    return wb, wf


def kernel(x, norm1_w, w_in, gla_gk_up, gla_gk_bias, gla_norm_w, hgrn_lb_logits, hgrn_norm_w,
           ssm_conv_w, ssm_conv_b, ssm_dt_bias, ssm_a_log, ssm_d, ssm_norm_w,
           w_br_gla, w_br_hgrn, w_br_ssm, w_out, norm2_w,
           router_group_w, router_group_b, router_expert_w, router_expert_b,
           moe_w_gate, moe_w_up, moe_w_down, final_norm_w,
           *, moe_blk=256, disp_tile=256, comb_tile=128, la_tb=512, ssd_tb=256, row_tile=256):
    bsz, t_len, d = x.shape
    n = bsz * t_len
    depth = w_in.shape[0]
    x2d = x.reshape(n, d).astype(F32)

    wb, wf = _prep_inproj(w_in)
    row = lambda a: a.astype(F32)[:, None, :]
    norm1 = row(norm1_w)
    norm2 = row(norm2_w)
    up = _pad_heads(jnp.pad(gla_gk_up, ((0, 0), (0, LANES - GLA_RANK), (0, 0))), HEADS, GLA_DK).astype(BF16)
    gb = row(_pad_heads(gla_gk_bias, HEADS, GLA_DK))
    gnw = row(gla_norm_w)
    lower = jnp.cumsum(jax.nn.softmax(hgrn_lb_logits.astype(F32), axis=0), axis=0)
    lb = lower - lower[0:1]
    loglb, log1m, onem = row(jnp.log(lb)), row(jnp.log1p(-lb)), row(1.0 - lb)
    hnw = row(hgrn_norm_w)
    cw = ssm_conv_w.astype(F32)
    cb = row(ssm_conv_b)
    dtb = row(_pad_last(ssm_dt_bias, LANES))
    alog = row(_pad_last(ssm_a_log, LANES))
    dvec = row(jnp.repeat(ssm_d, SSM_P, axis=-1))
    snw = row(ssm_norm_w)
    w1, w2, w3, wo = (w.astype(BF16) for w in (w_br_gla, w_br_hgrn, w_br_ssm, w_out))
    rw = _pad_last(jnp.concatenate([router_group_w, router_expert_w], axis=-1).astype(F32), LANES)
    rwh = rw.astype(BF16)
    rwl = (rw - rwh.astype(F32)).astype(BF16)
    rb = row(_pad_last(jnp.concatenate([router_group_b, router_expert_b], axis=-1), LANES))
    wg, wu, wd = (w.astype(BF16) for w in (moe_w_gate, moe_w_up, moe_w_down))
    fw = final_norm_w.astype(F32)[None, :]

    for l in range(depth):
        pb, pf = _inproj(x2d, norm1, wb, wf, l, tm=row_tile)
        yg = _lin_attn("gla", pb, pf, (up, gb, gnw), l, bsz, t_len, tb=la_tb)
        yh = _lin_attn("hgrn", pb, pf, (loglb, log1m, onem, hnw), l, bsz, t_len, tb=la_tb)
        ys = _ssd(pb, pf, cw, cb, dtb, alog, dvec, snw, l, bsz, t_len, tb=ssd_tb)
        x2d, h2, ri = _merge(yg, yh, ys, pb, x2d, w1, w2, w3, wo, norm2, rwh, rwl, rb, l, tm=row_tile)
        dest, block_e, nblk_used, rows = _route_slots(ri, moe_blk)
        xb = _dispatch(_tile_slots(dest, disp_tile), h2, jnp.zeros((rows, d), F32), disp_tile)
        yb = _ffn(block_e, nblk_used, xb, wg, wu, wd, l, moe_blk)
        x2d = _combine(_tile_slots(dest, comb_tile), yb, x2d, ri, fw, l == depth - 1, comb_tile)
    return x2d.reshape(bsz, t_len, d)
```

```python
import functools

import jax
import jax.numpy as jnp
import numpy as np
from jax import lax
from jax.experimental import pallas as pl
from jax.experimental.pallas import tpu as pltpu

F32 = jnp.float32
BF16 = jnp.bfloat16

D_MODEL = 1024
DEPTH = 4
NORM_EPS = 1e-6
HEADS = 4
HEAD_W = 128
MIX_W = HEADS * HEAD_W
GLA_DK = 64
GLA_RANK = 16
GLA_TEMP = 16.0
SSM_HEADS = 16
SSM_P = 64
SSM_GROUPS = 4
SSM_N = 128
SSM_CONV = 4
N_EXPERTS = 32
EXPERTS_PER_GROUP = 8
MOE_GROUPS = 4
MOE_FF = 512

LANES = 128
SUBLANES = 8

LA_CHUNK = 64
LA_SUB = 16
LA_GROUP = 4
LA_MILD = 60.0
SSD_CHUNK = 128

BF_COLS = dict(sxbc=(0, 2048), sz=(2048, 1024), gates=(3072, 3072), gv=(6144, 512), gr=(6656, 512),
               hq=(7168, 512), hi=(7680, 512), hg=(8192, 512), gq=(8704, 512), gk=(9216, 512))
BF_WIDTH = 9728
F_COLS = dict(hf=(0, 512), glow=(512, 128), sdt=(640, 128))
F_WIDTH = 768

VMEM_LIMIT = 56 * 1024 * 1024


def _cparams(sem):
    return pltpu.CompilerParams(dimension_semantics=sem, vmem_limit_bytes=VMEM_LIMIT)


def _log_sigmoid(x):
    return jnp.minimum(x, 0.0) - jnp.log1p(jnp.exp(-jnp.abs(x)))


def _softplus(x):
    return jnp.maximum(x, 0.0) + jnp.log1p(jnp.exp(-jnp.abs(x)))


def _silu(x):
    return x * jax.nn.sigmoid(x)


def _dot(a, b):
    return jnp.dot(a, b, preferred_element_type=F32)


def _dot_nt(a, b):
    return lax.dot_general(a, b, (((1,), (1,)), ((), ())), preferred_element_type=F32)


def _dot_tn(a, b):
    return lax.dot_general(a, b, (((0,), (0,)), ((), ())), preferred_element_type=F32)


def _split3(x):
    x1 = x.astype(BF16)
    r1 = x - x1.astype(F32)
    x2 = r1.astype(BF16)
    r2 = r1 - x2.astype(F32)
    return x1, x2, r2.astype(BF16)


def _cumsum_rows(tril, x):
    x1, x2, x3 = _split3(x)
    return _dot(tril, x1) + _dot(tril, x2) + _dot(tril, x3)


def _tril_ones(n):
    r = lax.broadcasted_iota(jnp.int32, (n, n), 0)
    c = lax.broadcasted_iota(jnp.int32, (n, n), 1)
    return r >= c


def _inproj_kernel(tm, tiles_per_seq, x_ref, nw_ref, wb_ref, wf_ref, cw_ref, cb_ref, ob_ref, of_ref, ubuf):
    i = pl.program_id(0)
    x = x_ref[...]
    h = x * lax.rsqrt(jnp.mean(x * x, axis=-1, keepdims=True) + NORM_EPS) * nw_ref[...]
    h = h.astype(BF16)
    step = 512
    conv_dim = BF_COLS["sxbc"][1]

    @pl.when(i % tiles_per_seq == 0)
    def _():
        ubuf[0:SUBLANES, :] = jnp.zeros((SUBLANES, conv_dim), F32)

    @pl.when(i % tiles_per_seq != 0)
    def _():
        ubuf[0:SUBLANES, :] = ubuf[tm:tm + SUBLANES, :]

    for c in range(0, conv_dim, step):
        ubuf[SUBLANES:SUBLANES + tm, c:c + step] = _dot(h, wb_ref[:, c:c + step])
    for c in range(0, conv_dim, step):
        acc = cb_ref[:, c:c + step] + ubuf[pl.ds(SUBLANES - SSM_CONV + 1, tm), c:c + step] * cw_ref[0:1, c:c + step]
        for w in range(1, SSM_CONV):
            acc = acc + ubuf[pl.ds(SUBLANES - SSM_CONV + 1 + w, tm), c:c + step] * cw_ref[w:w + 1, c:c + step]
        ob_ref[:, c:c + step] = _silu(acc).astype(ob_ref.dtype)
    for c in range(conv_dim, BF_WIDTH, step):
        ob_ref[:, c:c + step] = _dot(h, wb_ref[:, c:c + step]).astype(ob_ref.dtype)
    of_ref[...] = _dot(h, wf_ref[...])


def _inproj(x2d, nw, wb, wf, cw, cb, layer, t_len, tm=256):
    n = x2d.shape[0]
    conv_dim = BF_COLS["sxbc"][1]
    assert BF_COLS["sxbc"][0] == 0 and t_len % tm == 0
    return pl.pallas_call(
        functools.partial(_inproj_kernel, tm, t_len // tm),
        grid=(n // tm,),
        in_specs=[
            pl.BlockSpec((tm, D_MODEL), lambda i: (i, 0)),
            pl.BlockSpec((None, 1, D_MODEL), lambda i: (layer, 0, 0)),
            pl.BlockSpec((None, D_MODEL, BF_WIDTH), lambda i: (layer, 0, 0), pipeline_mode=pl.Buffered(1)),
            pl.BlockSpec((None, D_MODEL, F_WIDTH), lambda i: (layer, 0, 0), pipeline_mode=pl.Buffered(1)),
            pl.BlockSpec((None, SSM_CONV, conv_dim), lambda i: (layer, 0, 0)),
            pl.BlockSpec((None, 1, conv_dim), lambda i: (layer, 0, 0)),
        ],
        out_specs=[
            pl.BlockSpec((tm, BF_WIDTH), lambda i: (i, 0)),
            pl.BlockSpec((tm, F_WIDTH), lambda i: (i, 0)),
        ],
        out_shape=[jax.ShapeDtypeStruct((n, BF_WIDTH), BF16), jax.ShapeDtypeStruct((n, F_WIDTH), F32)],
        scratch_shapes=[pltpu.VMEM((tm + 2 * SUBLANES, conv_dim), F32)],
        compiler_params=_cparams(("arbitrary",)),
        name="inproj",
    )(x2d, nw, wb, wf, cw, cb)


def _la_chunk_head(qh, kh, vh, bh, hsl, kc_ref, bc_ref, st_ref, h):
    C = LA_CHUNK
    row8 = lax.broadcasted_iota(jnp.int32, (SUBLANES, 1), 0)
    colc = lax.broadcasted_iota(jnp.int32, (SUBLANES, C), 1)
    b_last = bh[C - 1:C, :]
    st = st_ref[h]
    o = _dot_nt((qh * jnp.exp(bh)).astype(BF16), st.astype(BF16))
    tiles = []
    for blk in range(C // LA_SUB):
        r0 = blk * LA_SUB
        if blk == 0:
            halves = [jnp.zeros((SUBLANES, C), F32), jnp.zeros((SUBLANES, C), F32)]
        else:
            bs = bh[r0:r0 + 1, :]
            qi = (qh[r0:r0 + LA_SUB] * jnp.exp(bh[r0:r0 + LA_SUB] - bs)).astype(BF16)
            kj = kh[0:r0] * jnp.exp(bs - bh[0:r0])
            kj = jnp.concatenate([kj, jnp.zeros((C - r0, HEAD_W), F32)], axis=0).astype(BF16)
            s = _dot_nt(qi, kj)
            halves = [s[0:SUBLANES], s[SUBLANES:LA_SUB]]
        for j in range(LA_SUB):
            bj = bc_ref[r0 + j:r0 + j + 1, hsl]
            kj = kc_ref[r0 + j:r0 + j + 1, hsl]
            for half in range(2):
                if j >= SUBLANES * (half + 1):
                    continue
                rs = slice(r0 + SUBLANES * half, r0 + SUBLANES * (half + 1))
                d = bh[rs] - bj
                p = qh[rs] * kj
                if j >= SUBLANES * half:
                    m = row8 >= (j - SUBLANES * half)
                    p = jnp.where(m, p * jnp.exp(jnp.where(m, d, 0.0)), 0.0)
                else:
                    p = p * jnp.exp(d)
                r = jnp.sum(p, axis=-1, keepdims=True)
                halves[half] = jnp.where(colc == (r0 + j), r, halves[half])
        tiles += halves
    a = jnp.concatenate(tiles, axis=0).astype(BF16)
    o = o + _dot(a, vh)
    kte = (kh * jnp.exp(b_last - bh)).astype(BF16)
    st_ref[h] = st * jnp.exp(b_last) + _dot_tn(vh, kte)
    return o


def _la_mild_prep(qh, kh, vh, bh, causal):
    C = LA_CHUNK
    b_last = bh[C - 1:C, :]
    b_mid = bh[C // 2 - 1:C // 2, :]
    qm = qh * jnp.exp(bh - b_mid)
    ke = (kh * jnp.exp(b_mid - bh)).astype(BF16)
    a = jnp.where(causal, _dot_nt(qm.astype(BF16), ke), 0.0).astype(BF16)
    kte = (kh * jnp.exp(b_last - bh)).astype(BF16)
    return (qm * jnp.exp(b_mid)).astype(BF16), _dot(a, vh), _dot_tn(vh, kte), jnp.exp(b_last)


def _la_kernel(mode, nchunks, *refs):
    if mode == "gla":
        (q_ref, k_ref, v_ref, gate_ref, gl_ref, up_ref, gb_ref, nw_ref,
         o_ref, st_ref, q_s, k_s, b_s, o_s, kc_ref, bc_ref) = refs
    else:
        (q_ref, v_ref, gate_ref, f_ref, loglb_ref, log1m_ref, onem_ref, nw_ref,
         o_ref, st_ref, q_s, k_s, b_s, o_s, kc_ref, bc_ref) = refs

    @pl.when(pl.program_id(1) == 0)
    def _():
        st_ref[...] = jnp.zeros_like(st_ref)

    C = LA_CHUNK
    causal = _tril_ones(C)
    tril = causal.astype(BF16)

    if mode == "gla":
        q_s[...] = q_ref[...].astype(F32) * (GLA_DK ** -0.5)
        k_s[...] = k_ref[...].astype(F32)
        logit = _dot(gl_ref[...].astype(BF16), up_ref[...]) + gb_ref[...]
        g = _log_sigmoid(logit) * (1.0 / GLA_TEMP)
    else:
        q_s[...] = _silu(q_ref[...].astype(F32)) * (HEAD_W ** -0.5)
        f = f_ref[...]
        u = loglb_ref[...]
        w = log1m_ref[...] + _log_sigmoid(f)
        g = jnp.maximum(u, w) + jnp.log1p(jnp.exp(-jnp.abs(u - w)))
        k_s[...] = onem_ref[...] * jax.nn.sigmoid(-f)
    spread = jnp.zeros((1, MIX_W), F32)
    for c in range(nchunks):
        b = _cumsum_rows(tril, g[c * C:(c + 1) * C])
        b_s[c * C:(c + 1) * C, :] = b
        b_mid = b[C // 2 - 1:C // 2]
        spread = jnp.maximum(spread, jnp.maximum(b[0:1] - b_mid, b_mid - b[C - 1:C]))
    mild = jnp.max(spread) <= LA_MILD

    @pl.when(mild)
    def _():
        def body(gi, carry):
            base = pl.multiple_of(gi * (LA_GROUP * C), LA_GROUP * C)
            for h in range(HEADS):
                hsl = slice(h * HEAD_W, (h + 1) * HEAD_W)
                rows = [pl.ds(base + j * C, C) for j in range(LA_GROUP)]
                prep = [_la_mild_prep(q_s[rs, hsl], k_s[rs, hsl], v_ref[rs, hsl], b_s[rs, hsl], causal)
                        for rs in rows]
                st = st_ref[h]
                for rs, (qe, o_intra, inc, dec) in zip(rows, prep):
                    o_s[rs, hsl] = o_intra + _dot_nt(qe, st.astype(BF16))
                    st = st * dec + inc
                st_ref[h] = st
            return carry
        lax.fori_loop(0, nchunks // LA_GROUP, body, 0)

    @pl.when(jnp.logical_not(mild))
    def _():
        def body(c, carry):
            rs = pl.ds(pl.multiple_of(c * C, C), C)
            kc_ref[...] = k_s[rs, :]
            bc_ref[...] = b_s[rs, :]
            for h in range(HEADS):
                hsl = slice(h * HEAD_W, (h + 1) * HEAD_W)
                o_s[rs, hsl] = _la_chunk_head(q_s[rs, hsl], k_s[rs, hsl], v_ref[rs, hsl], b_s[rs, hsl],
                                              hsl, kc_ref, bc_ref, st_ref, h)
            return carry
        lax.fori_loop(0, nchunks, body, 0)

    for h in range(HEADS):
        hsl = slice(h * HEAD_W, (h + 1) * HEAD_W)
        o = o_s[:, hsl]
        y = o * lax.rsqrt(jnp.mean(o * o, axis=-1, keepdims=True) + NORM_EPS) * nw_ref[:, hsl]
        o_ref[:, hsl] = (y * _silu(gate_ref[:, hsl].astype(F32))).astype(o_ref.dtype)


def _col_spec(tb, tblocks, width, off):
    assert off % width == 0
    cb = off // width
    return pl.BlockSpec((tb, width), lambda b, t: (b * tblocks + t, cb))


def _row_param(width, layer):
    return pl.BlockSpec((None, 1, width), lambda b, t: (layer, 0, 0))


def _lin_attn(mode, pb, pf, params, layer, bsz, t_len, tb=512):
    n = bsz * t_len
    tblocks = t_len // tb
    sp = functools.partial(_col_spec, tb, tblocks)
    if mode == "gla":
        up, gb, nw = params
        ins = [pb, pb, pb, pb, pf, up, gb, nw]
        specs = [sp(MIX_W, BF_COLS["gq"][0]), sp(MIX_W, BF_COLS["gk"][0]), sp(MIX_W, BF_COLS["gv"][0]),
                 sp(MIX_W, BF_COLS["gr"][0]), sp(LANES, F_COLS["glow"][0]),
                 pl.BlockSpec((None, LANES, MIX_W), lambda b, t: (layer, 0, 0)),
                 _row_param(MIX_W, layer), _row_param(MIX_W, layer)]
    else:
        loglb, log1m, onem, nw = params
        ins = [pb, pb, pb, pf, loglb, log1m, onem, nw]
        specs = [sp(MIX_W, BF_COLS["hq"][0]), sp(MIX_W, BF_COLS["hi"][0]), sp(MIX_W, BF_COLS["hg"][0]),
                 sp(MIX_W, F_COLS["hf"][0]),
                 _row_param(MIX_W, layer), _row_param(MIX_W, layer), _row_param(MIX_W, layer),
                 _row_param(MIX_W, layer)]
    return pl.pallas_call(
        functools.partial(_la_kernel, mode, tb // LA_CHUNK),
        grid=(bsz, tblocks),
        in_specs=specs,
        out_specs=pl.BlockSpec((tb, MIX_W), lambda b, t: (b * tblocks + t, 0)),
        out_shape=jax.ShapeDtypeStruct((n, MIX_W), BF16),
        scratch_shapes=([pltpu.VMEM((HEADS, HEAD_W, HEAD_W), F32)] + [pltpu.VMEM((tb, MIX_W), F32)] * 4
                        + [pltpu.VMEM((LA_CHUNK, MIX_W), F32)] * 2),
        compiler_params=_cparams(("parallel", "arbitrary")),
        name="lin_attn_" + mode,
    )(*ins)


SSD_B_PARTS = 3
SSD_DT_PARTS = 2
SSD_SEL_COLS = SSM_HEADS * LANES + (SSM_HEADS // 2) * LANES


def _ssd_select():
    k = np.arange(LANES)[:, None]
    n = np.arange(SSD_SEL_COLS)[None, :]
    nb = SSM_HEADS * LANES
    h = n // LANES
    sel_b = (n < nb) & (k < SSD_B_PARTS * SSM_HEADS) & (k % SSM_HEADS == h)
    hh = 2 * ((n - nb) // LANES) + ((n - nb) % LANES >= SSM_P)
    kd = k - SSD_B_PARTS * SSM_HEADS
    sel_d = (n >= nb) & (kd >= 0) & (kd < SSD_DT_PARTS * SSM_HEADS) & (kd % SSM_HEADS == hh)
    return jnp.asarray((sel_b | sel_d).astype(np.float32), dtype=BF16)


def _ssd_kernel(nchunks, xbc_ref, z_ref, dt_ref, sel_ref, dtb_ref, alog_ref, dvec_ref, nw_ref, o_ref, st_ref):
    @pl.when(pl.program_id(1) == 0)
    def _():
        st_ref[...] = jnp.zeros_like(st_ref)

    C = SSD_CHUNK
    causal = _tril_ones(C)
    tril = causal.astype(BF16)
    lane = lax.broadcasted_iota(jnp.int32, (C, LANES), 1)
    row = lax.broadcasted_iota(jnp.int32, (C, LANES), 0)
    lo_lane = lane < SSM_P
    lo_row = row < SSM_P
    head_lane = lane < SSM_HEADS
    a_neg = -jnp.exp(alog_ref[...])
    d_inner = SSM_HEADS * SSM_P
    gn = SSM_GROUPS * SSM_N
    hpg = SSM_HEADS // SSM_GROUPS

    def body(c, carry):
        rs = pl.ds(pl.multiple_of(c * C, C), C)
        dt = _softplus(dt_ref[rs, :] + dtb_ref[...])
        b_col = _cumsum_rows(tril, dt * a_neg)
        b_row = b_col.T
        dt_row = dt.T
        e_last = jnp.exp(b_col[C - 1:C, :])
        d1 = dt.astype(BF16)
        parts = list(_split3(b_col)) + [d1, (dt - d1.astype(F32)).astype(BF16)]
        packed = jnp.zeros((C, LANES), F32)
        for idx, part in enumerate(parts):
            v = jnp.where(head_lane, part.astype(F32), 0.0)
            packed = packed + (pltpu.roll(v, SSM_HEADS * idx, 1) if idx else v)
        bc = _dot(packed.astype(BF16), sel_ref[...])
        for g in range(SSM_GROUPS):
            bg = xbc_ref[rs, d_inner + g * SSM_N:d_inner + (g + 1) * SSM_N]
            cg = xbc_ref[rs, d_inner + gn + g * SSM_N:d_inner + gn + (g + 1) * SSM_N]
            cbm = _dot_nt(cg, bg)
            ys = []
            for pp in range(hpg // 2):
                p = g * (hpg // 2) + pp
                ha, hb = 2 * p, 2 * p + 1
                psl = slice(p * LANES, (p + 1) * LANES)
                x2b = xbc_ref[rs, psl]
                x2 = x2b.astype(F32)
                b_a = bc[:, ha * LANES:(ha + 1) * LANES]
                b_b = bc[:, hb * LANES:(hb + 1) * LANES]
                dt_sel = bc[:, (SSM_HEADS + p) * LANES:(SSM_HEADS + p + 1) * LANES]

                def wmat(h, b_i):
                    seg = b_i - b_row[h:h + 1, :]
                    dec = jnp.where(causal, jnp.exp(jnp.where(causal, seg, 0.0)), 0.0)
                    return (dec * cbm * dt_row[h:h + 1, :]).astype(BF16)

                y2 = jnp.where(lo_lane, _dot(wmat(ha, b_a), x2b), _dot(wmat(hb, b_b), x2b))
                stp = st_ref[p]
                b_sel = jnp.where(lo_lane, b_a, b_b)
                y2 = y2 + _dot_nt(cg, stp.astype(BF16)) * jnp.exp(b_sel)
                wsel = jnp.exp(b_sel[C - 1:C, :] - b_sel) * dt_sel
                upd = _dot_tn((x2 * wsel).astype(BF16), bg)
                dsel = jnp.where(lo_row, e_last[:, ha:ha + 1], e_last[:, hb:hb + 1])
                st_ref[p] = stp * dsel + upd
                y2 = y2 + dvec_ref[:, psl] * x2
                y2 = y2 * _silu(z_ref[rs, psl].astype(F32))
                ys.append(y2)
            yg = jnp.concatenate(ys, axis=-1)
            var = jnp.mean(yg * yg, axis=-1, keepdims=True)
            gsl = slice(g * 2 * LANES, (g + 1) * 2 * LANES)
            o_ref[rs, gsl] = (yg * lax.rsqrt(var + NORM_EPS) * nw_ref[:, gsl]).astype(o_ref.dtype)
        return carry

    lax.fori_loop(0, nchunks, body, 0)


def _ssd(pb, pf, sel, dtb, alog, dvec, nw, layer, bsz, t_len, tb=256):
    n = bsz * t_len
    tblocks = t_len // tb
    sp = functools.partial(_col_spec, tb, tblocks)
    conv_dim = SSM_HEADS * SSM_P + 2 * SSM_GROUPS * SSM_N
    d_inner = SSM_HEADS * SSM_P
    return pl.pallas_call(
        functools.partial(_ssd_kernel, tb // SSD_CHUNK),
        grid=(bsz, tblocks),
        in_specs=[sp(conv_dim, BF_COLS["sxbc"][0]), sp(d_inner, BF_COLS["sz"][0]), sp(LANES, F_COLS["sdt"][0]),
                  pl.BlockSpec((LANES, SSD_SEL_COLS), lambda b, t: (0, 0)),
                  _row_param(LANES, layer), _row_param(LANES, layer),
                  _row_param(d_inner, layer), _row_param(d_inner, layer)],
        out_specs=pl.BlockSpec((tb, d_inner), lambda b, t: (b * tblocks + t, 0)),
        out_shape=jax.ShapeDtypeStruct((n, d_inner), BF16),
        scratch_shapes=[pltpu.VMEM((SSM_HEADS // 2, 2 * SSM_P, SSM_N), F32)],
        compiler_params=_cparams(("parallel", "arbitrary")),
        name="ssd",
    )(pb, pb, pf, sel, dtb, alog, dvec, nw)


def _merge_kernel(yg_ref, yh_ref, ys_ref, gates_ref, x_ref, w1_ref, w2_ref, w3_ref, wo_ref,
                  n2_ref, rwh_ref, rwl_ref, rb_ref, xo_ref, h2_ref, ri_ref):
    g = jax.nn.sigmoid(gates_ref[...].astype(F32))
    m = (g[:, 0:D_MODEL] * _dot(yg_ref[...], w1_ref[...])
         + g[:, D_MODEL:2 * D_MODEL] * _dot(yh_ref[...], w2_ref[...])
         + g[:, 2 * D_MODEL:3 * D_MODEL] * _dot(ys_ref[...], w3_ref[...]))
    xn = x_ref[...] + _dot(m.astype(BF16), wo_ref[...])
    xo_ref[...] = xn
    h2 = xn * lax.rsqrt(jnp.mean(xn * xn, axis=-1, keepdims=True) + NORM_EPS) * n2_ref[...]
    h2_ref[...] = h2
    hi = h2.astype(BF16)
    lo = (h2 - hi.astype(F32)).astype(BF16)
    logits = _dot(hi, rwh_ref[...]) + _dot(lo, rwh_ref[...]) + _dot(hi, rwl_ref[...]) + rb_ref[...]

    lane = lax.broadcasted_iota(jnp.int32, logits.shape, 1)
    lanef = lane.astype(F32)
    neg = jnp.float32(-1e30)
    big = jnp.float32(1e9)
    gmask = lane < MOE_GROUPS
    gl = jnp.where(gmask, logits, neg)
    gexp = jnp.where(gmask, jnp.exp(gl - jnp.max(gl, axis=-1, keepdims=True)), 0.0)
    gprob = gexp / jnp.sum(gexp, axis=-1, keepdims=True)
    g_p = jnp.max(gprob, axis=-1, keepdims=True)
    g_idx = jnp.min(jnp.where(gmask & (gprob == g_p), lanef, big), axis=-1, keepdims=True)
    lo_l = MOE_GROUPS + EXPERTS_PER_GROUP * g_idx
    emask = (lanef >= lo_l) & (lanef < lo_l + EXPERTS_PER_GROUP)
    el = jnp.where(emask, logits, neg)
    eexp = jnp.where(emask, jnp.exp(el - jnp.max(el, axis=-1, keepdims=True)), 0.0)
    eprob = eexp / jnp.sum(eexp, axis=-1, keepdims=True)
    p1 = jnp.max(jnp.where(emask, eprob, -1.0), axis=-1, keepdims=True)
    i1 = jnp.min(jnp.where(emask & (eprob == p1), lanef, big), axis=-1, keepdims=True)
    rest = emask & (lanef != i1)
    p2 = jnp.max(jnp.where(rest, eprob, -1.0), axis=-1, keepdims=True)
    i2 = jnp.min(jnp.where(rest & (eprob == p2), lanef, big), axis=-1, keepdims=True)
    den = p1 + p2
    w_a = g_p * (p1 / den)
    w_b = g_p * (p2 / den)
    ri_ref[...] = jnp.where(lane == 0, i1 - MOE_GROUPS,
                            jnp.where(lane == 1, i2 - MOE_GROUPS,
                                      jnp.where(lane == 2, w_a, jnp.where(lane == 3, w_b, 0.0))))


def _merge(yg, yh, ys, pb, x2d, w1, w2, w3, wo, n2, rwh, rwl, rb, layer, tm=256):
    n = x2d.shape[0]
    gates_blk = BF_COLS["gates"][0] // BF_COLS["gates"][1]

    def wspec(k):
        return pl.BlockSpec((None, k, D_MODEL), lambda i: (layer, 0, 0))

    return pl.pallas_call(
        _merge_kernel,
        grid=(n // tm,),
        in_specs=[pl.BlockSpec((tm, MIX_W), lambda i: (i, 0)),
                  pl.BlockSpec((tm, MIX_W), lambda i: (i, 0)),
                  pl.BlockSpec((tm, D_MODEL), lambda i: (i, 0)),
                  pl.BlockSpec((tm, 3 * D_MODEL), lambda i: (i, gates_blk)),
                  pl.BlockSpec((tm, D_MODEL), lambda i: (i, 0)),
                  wspec(MIX_W), wspec(MIX_W), wspec(D_MODEL), wspec(D_MODEL),
                  pl.BlockSpec((None, 1, D_MODEL), lambda i: (layer, 0, 0)),
                  pl.BlockSpec((None, D_MODEL, LANES), lambda i: (layer, 0, 0)),
                  pl.BlockSpec((None, D_MODEL, LANES), lambda i: (layer, 0, 0)),
                  pl.BlockSpec((None, 1, LANES), lambda i: (layer, 0, 0))],
        out_specs=[pl.BlockSpec((tm, D_MODEL), lambda i: (i, 0)),
                   pl.BlockSpec((tm, D_MODEL), lambda i: (i, 0)),
                   pl.BlockSpec((tm, LANES), lambda i: (i, 0))],
        out_shape=[jax.ShapeDtypeStruct((n, D_MODEL), F32),
                   jax.ShapeDtypeStruct((n, D_MODEL), F32),
                   jax.ShapeDtypeStruct((n, LANES), F32)],
        compiler_params=_cparams(("parallel",)),
        name="merge_router",
    )(yg, yh, ys, pb, x2d, w1, w2, w3, wo, n2, rwh, rwl, rb)


def _dispatch_kernel(tp, nsteps, blk, idx_hbm, pstart_ref, pend_ref, h_ref, xb_out, idx_smem, zbuf, isem, rsem,
                     zsem):
    i = pl.program_id(0)
    slot = i % 2

    def idx_copy(step, s):
        return pltpu.make_async_copy(idx_hbm.at[step], idx_smem.at[s], isem.at[s])

    def row_copy(r, dst):
        return pltpu.make_async_copy(h_ref.at[pl.ds(r, 1)], xb_out.at[pl.ds(dst, 1)], rsem)

    def zero_copy(start):
        if not isinstance(start, int):
            start = pl.multiple_of(start, blk)
        return pltpu.make_async_copy(zbuf, xb_out.at[pl.ds(start, blk)], zsem)

    @pl.when(i == 0)
    def _():
        idx_copy(0, 0).start()
        if nsteps > 1:
            idx_copy(1, 1).start()
        zbuf[...] = jnp.zeros_like(zbuf)
        total = xb_out.shape[0]
        todo = [(pend_ref[e] > pstart_ref[e], pend_ref[e] - blk) for e in range(N_EXPERTS)]
        todo += [(start >= pend_ref[N_EXPERTS - 1], start) for start in range(total - N_EXPERTS * blk, total, blk)]
        for cond, start in todo:
            @pl.when(cond)
            def _():
                zero_copy(start).start()
        for cond, start in todo:
            @pl.when(cond)
            def _():
                zero_copy(start).wait()

    idx_copy(i, slot).wait()

    for r in range(tp):
        for k in range(2):
            row_copy(r, idx_smem[slot, k * tp + r]).start()

    @pl.when(i + 2 < nsteps)
    def _():
        idx_copy(i + 2, slot).start()

    def drain(r, carry):
        for k in range(2):
            row_copy(0, 0).wait()
        return carry

    lax.fori_loop(0, tp, drain, 0, unroll=8)


def _dispatch(dest_tiles, pstart, pend, h2, rows, tp, blk):
    n = h2.shape[0]
    return pl.pallas_call(
        functools.partial(_dispatch_kernel, tp, n // tp, blk),
        grid=(n // tp,),
        in_specs=[pl.BlockSpec(memory_space=pl.ANY),
                  pl.BlockSpec(memory_space=pltpu.SMEM),
                  pl.BlockSpec(memory_space=pltpu.SMEM),
                  pl.BlockSpec((tp, D_MODEL), lambda i: (i, 0))],
        out_specs=pl.BlockSpec(memory_space=pl.ANY),
        out_shape=jax.ShapeDtypeStruct((rows, D_MODEL), F32),
        scratch_shapes=[pltpu.SMEM((2, 2 * tp), jnp.int32),
                        pltpu.VMEM((blk, D_MODEL), F32),
                        pltpu.SemaphoreType.DMA((2,)),
                        pltpu.SemaphoreType.DMA(()),
                        pltpu.SemaphoreType.DMA(())],
        compiler_params=_cparams(("arbitrary",)),
        name="moe_dispatch",
    )(dest_tiles, pstart, pend, h2)


def _ffn_kernel(be_ref, nb_ref, x_ref, wg_ref, wu_ref, wd_ref, o_ref, wg_s, wu_s, wd_s):
    i = pl.program_id(0)
    used = i < nb_ref[0]

    @pl.when(used & ((i == 0) | (be_ref[i] != be_ref[jnp.maximum(i - 1, 0)])))
    def _():
        wg_s[...] = wg_ref[...].astype(BF16)
        wu_s[...] = wu_ref[...].astype(BF16)
        wd_s[...] = wd_ref[...].astype(BF16)

    @pl.when(used)
    def _():
        x = x_ref[...].astype(BF16)
        hid = _silu(_dot(x, wg_s[...])) * _dot(x, wu_s[...])
        o_ref[...] = _dot(hid.astype(BF16), wd_s[...])

    @pl.when(jnp.logical_not(used))
    def _():
        o_ref[...] = jnp.zeros_like(o_ref)


def _ffn(block_e, nblk_used, xb, wg, wu, wd, layer, blk):
    rows = xb.shape[0]

    def x_map(i, be, nb):
        return (jnp.minimum(i, nb[0] - 1), 0)

    def w_map(i, be, nb):
        return (layer, be[jnp.minimum(i, nb[0] - 1)], 0, 0)

    grid_spec = pltpu.PrefetchScalarGridSpec(
        num_scalar_prefetch=2,
        grid=(rows // blk,),
        in_specs=[pl.BlockSpec((blk, D_MODEL), x_map),
                  pl.BlockSpec((None, None, D_MODEL, MOE_FF), w_map),
                  pl.BlockSpec((None, None, D_MODEL, MOE_FF), w_map),
                  pl.BlockSpec((None, None, MOE_FF, D_MODEL), w_map)],
        out_specs=pl.BlockSpec((blk, D_MODEL), lambda i, be, nb: (i, 0)),
        scratch_shapes=[pltpu.VMEM((D_MODEL, MOE_FF), BF16),
                        pltpu.VMEM((D_MODEL, MOE_FF), BF16),
                        pltpu.VMEM((MOE_FF, D_MODEL), BF16)],
    )
    return pl.pallas_call(
        _ffn_kernel,
        grid_spec=grid_spec,
        out_shape=jax.ShapeDtypeStruct((rows, D_MODEL), F32),
        compiler_params=_cparams(("arbitrary",)),
        name="moe_ffn",
    )(block_e, nblk_used, xb, wg, wu, wd)


def _combine_kernel(tc, nsteps, final, idx_hbm, yb_hbm, x_ref, w_ref, fw_ref, o_ref, idx_smem, ybuf, isem, rsem):
    i = pl.program_id(0)
    slot = i % 2

    def idx_copy(step, s):
        return pltpu.make_async_copy(idx_hbm.at[step], idx_smem.at[s], isem.at[s])

    def row_copy(src, r, s):
        return pltpu.make_async_copy(yb_hbm.at[pl.ds(src, 1)], ybuf.at[s, pl.ds(r, 1)], rsem.at[s])

    def issue_rows(s):
        for r in range(2 * tc):
            row_copy(idx_smem[s, r], r, s).start()

    @pl.when(i == 0)
    def _():
        idx_copy(0, 0).start()
        idx_copy(0, 0).wait()
        issue_rows(0)
        if nsteps > 1:
            idx_copy(1, 1).start()

    @pl.when(i + 1 < nsteps)
    def _():
        idx_copy(i + 1, 1 - slot).wait()
        issue_rows(1 - slot)

    @pl.when(i + 2 < nsteps)
    def _():
        idx_copy(i + 2, slot).start()

    def drain(r, carry):
        row_copy(0, 0, slot).wait()
        return carry

    lax.fori_loop(0, 2 * tc, drain, 0, unroll=8)
    w = w_ref[...]
    out = x_ref[...] + w[:, 2:3] * ybuf[slot, 0:tc, :] + w[:, 3:4] * ybuf[slot, tc:2 * tc, :]
    if final:
        out = out * lax.rsqrt(jnp.mean(out * out, axis=-1, keepdims=True) + NORM_EPS) * fw_ref[...]
    o_ref[...] = out


def _combine(dest_tiles, yb, x2d, ri, fw, final, tc):
    n = x2d.shape[0]
    return pl.pallas_call(
        functools.partial(_combine_kernel, tc, n // tc, final),
        grid=(n // tc,),
        in_specs=[pl.BlockSpec(memory_space=pl.ANY),
                  pl.BlockSpec(memory_space=pl.ANY),
                  pl.BlockSpec((tc, D_MODEL), lambda i: (i, 0)),
                  pl.BlockSpec((tc, LANES), lambda i: (i, 0)),
                  pl.BlockSpec((1, D_MODEL), lambda i: (0, 0))],
        out_specs=pl.BlockSpec((tc, D_MODEL), lambda i: (i, 0)),
        out_shape=jax.ShapeDtypeStruct((n, D_MODEL), F32),
        scratch_shapes=[pltpu.SMEM((2, 2 * tc), jnp.int32),
                        pltpu.VMEM((2, 2 * tc, D_MODEL), F32),
                        pltpu.SemaphoreType.DMA((2,)),
                        pltpu.SemaphoreType.DMA((2,))],
        compiler_params=_cparams(("arbitrary",)),
        name="moe_combine",
    )(dest_tiles, yb, x2d, ri, fw)


def _route_slots(ri, blk):
    n = ri.shape[0]
    e = ri[:, 0:2].astype(jnp.int32)
    flat_e = e.reshape(-1)
    onehot = (flat_e[:, None] == jnp.arange(N_EXPERTS, dtype=jnp.int32)[None, :]).astype(jnp.int32)
    csum = jnp.cumsum(onehot, axis=0)
    pos = jnp.sum(csum * onehot, axis=1) - 1
    counts = csum[-1]
    padded = (counts + blk - 1) // blk * blk
    pend = jnp.cumsum(padded)
    pstart = pend - padded
    dest = (jnp.sum(onehot * pstart[None, :], axis=1) + pos).astype(jnp.int32).reshape(n, 2)
    rows = 2 * n + N_EXPERTS * blk
    nblk = rows // blk
    block_start = jnp.arange(nblk, dtype=jnp.int32) * blk
    block_e = jnp.minimum(jnp.sum((block_start[:, None] >= pend[None, :]).astype(jnp.int32), axis=1),
                          N_EXPERTS - 1).astype(jnp.int32)
    nblk_used = (pend[-1] // blk).astype(jnp.int32).reshape(1)
    return dest, block_e, nblk_used, pstart.astype(jnp.int32), pend.astype(jnp.int32), rows


def _tile_slots(dest, tile):
    n = dest.shape[0]
    return dest.reshape(n // tile, tile, 2).transpose(0, 2, 1).reshape(n // tile, 2 * tile)


def _pad_heads(w, heads, width):
    lead = w.shape[:-1]
    w = w.reshape(lead + (heads, width))
    w = jnp.pad(w, [(0, 0)] * len(lead) + [(0, 0), (0, HEAD_W - width)])
    return w.reshape(lead + (heads * HEAD_W,))


def _pad_last(w, width):
    return jnp.pad(w, [(0, 0)] * (w.ndim - 1) + [(0, width - w.shape[-1])])


def _prep_inproj(w_in):
    gla_kw, gla_vw, hw = HEADS * GLA_DK, MIX_W, MIX_W
    d_inner = SSM_HEADS * SSM_P
    conv_dim = d_inner + 2 * SSM_GROUPS * SSM_N
    sizes = (gla_kw, gla_kw, gla_vw, GLA_RANK, gla_vw, hw, hw, hw, hw, d_inner, conv_dim, SSM_HEADS, 3 * D_MODEL)
    offs = [0]
    for s in sizes:
        offs.append(offs[-1] + s)
    seg = {name: w_in[..., offs[i]:offs[i + 1]] for i, name in enumerate(
        ("gq", "gk", "gv", "glow", "gr", "hq", "hf", "hi", "hg", "sz", "sxbc", "sdt", "gates"))}
    seg["gq"] = _pad_heads(seg["gq"], HEADS, GLA_DK)
    seg["gk"] = _pad_heads(seg["gk"], HEADS, GLA_DK)
    seg["glow"] = _pad_last(seg["glow"], LANES)
    seg["sdt"] = _pad_last(seg["sdt"], LANES)
    wb = jnp.concatenate([seg[k] for k in sorted(BF_COLS, key=lambda k: BF_COLS[k][0])], axis=-1).astype(BF16)
    wf = jnp.concatenate([seg[k] for k in sorted(F_COLS, key=lambda k: F_COLS[k][0])], axis=-1).astype(BF16)
    return wb, wf


def kernel(x, norm1_w, w_in, gla_gk_up, gla_gk_bias, gla_norm_w, hgrn_lb_logits, hgrn_norm_w,
           ssm_conv_w, ssm_conv_b, ssm_dt_bias, ssm_a_log, ssm_d, ssm_norm_w,
           w_br_gla, w_br_hgrn, w_br_ssm, w_out, norm2_w,
           router_group_w, router_group_b, router_expert_w, router_expert_b,
           moe_w_gate, moe_w_up, moe_w_down, final_norm_w,
           *, moe_blk=256, disp_tile=256, comb_tile=128, la_tb=512, ssd_tb=256, row_tile=256):
    bsz, t_len, d = x.shape
    n = bsz * t_len
    depth = w_in.shape[0]
    x2d = x.reshape(n, d).astype(F32)

    wb, wf = _prep_inproj(w_in)
    row = lambda a: a.astype(F32)[:, None, :]
    norm1 = row(norm1_w)
    norm2 = row(norm2_w)
    up = _pad_heads(jnp.pad(gla_gk_up, ((0, 0), (0, LANES - GLA_RANK), (0, 0))), HEADS, GLA_DK).astype(BF16)
    gb = row(_pad_heads(gla_gk_bias, HEADS, GLA_DK))
    gnw = row(gla_norm_w)
    lower = jnp.cumsum(jax.nn.softmax(hgrn_lb_logits.astype(F32), axis=0), axis=0)
    lb = lower - lower[0:1]
    loglb, log1m, onem = row(jnp.log(lb)), row(jnp.log1p(-lb)), row(1.0 - lb)
    hnw = row(hgrn_norm_w)
    cw = ssm_conv_w.astype(F32)
    cb = row(ssm_conv_b)
    dtb = row(_pad_last(ssm_dt_bias, LANES))
    alog = row(_pad_last(ssm_a_log, LANES))
    dvec = row(jnp.repeat(ssm_d, SSM_P, axis=-1))
    snw = row(ssm_norm_w)
    sel = _ssd_select()
    w1, w2, w3, wo = (w.astype(BF16) for w in (w_br_gla, w_br_hgrn, w_br_ssm, w_out))
    rw = _pad_last(jnp.concatenate([router_group_w, router_expert_w], axis=-1).astype(F32), LANES)
    rwh = rw.astype(BF16)
    rwl = (rw - rwh.astype(F32)).astype(BF16)
    rb = row(_pad_last(jnp.concatenate([router_group_b, router_expert_b], axis=-1), LANES))
    fw =final_norm_w.astype(F32)[None, :]

    for l in range(depth):
        pb, pf = _inproj(x2d, norm1, wb, wf, cw, cb, l, t_len, tm=row_tile)
        yg = _lin_attn("gla", pb, pf, (up, gb, gnw), l, bsz, t_len, tb=la_tb)
        yh = _lin_attn("hgrn", pb, pf, (loglb, log1m, onem, hnw), l, bsz, t_len, tb=la_tb)
        ys = _ssd(pb, pf, sel, dtb, alog, dvec, snw, l, bsz, t_len, tb=ssd_tb)
        x2d, h2, ri = _merge(yg, yh, ys, pb, x2d, w1, w2, w3, wo, norm2, rwh, rwl, rb, l, tm=row_tile)
        dest, block_e, nblk_used, pstart, pend, rows = _route_slots(ri, moe_blk)
        xb = _dispatch(_tile_slots(dest, disp_tile), pstart, pend, h2, rows, disp_tile, moe_blk)
        yb = _ffn(block_e, nblk_used, xb, moe_w_gate, moe_w_up, moe_w_down, l, moe_blk)
        x2d = _combine(_tile_slots(dest, comb_tile), yb, x2d, ri, fw, l == depth - 1, comb_tile)
    return x2d.reshape(bsz, t_len, d)
```

```python
import functools

import jax
import jax.numpy as jnp
import numpy as np
from jax import lax
from jax.experimental import pallas as pl
from jax.experimental.pallas import tpu as pltpu

F32 = jnp.float32
BF16 = jnp.bfloat16

D_MODEL = 1024
DEPTH = 4
NORM_EPS = 1e-6
HEADS = 4
HEAD_W = 128
MIX_W = HEADS * HEAD_W
GLA_DK = 64
GLA_RANK = 16
GLA_TEMP = 16.0
SSM_HEADS = 16
SSM_P = 64
SSM_GROUPS = 4
SSM_N = 128
SSM_CONV = 4
N_EXPERTS = 32
EXPERTS_PER_GROUP = 8
MOE_GROUPS = 4
MOE_FF = 512

LANES = 128
SUBLANES = 8

LA_CHUNK = 64
LA_SUB = 16
LA_GROUP = 4
LA_MILD = 60.0
SSD_CHUNK = 128

BF_COLS = dict(sxbc=(0, 2048), sz=(2048, 1024), gates=(3072, 3072), gv=(6144, 512), gr=(6656, 512),
               hq=(7168, 512), hi=(7680, 512), hg=(8192, 512), gq=(8704, 512), gk=(9216, 512))
BF_WIDTH = 9728
F_COLS = dict(hf=(0, 512), glow=(512, 128), sdt=(640, 128))
F_WIDTH = 768

VMEM_LIMIT = 56 * 1024 * 1024


def _cparams(sem):
    return pltpu.CompilerParams(dimension_semantics=sem, vmem_limit_bytes=VMEM_LIMIT)


def _log1p_exp_neg_abs(x):
    return jnp.log(1.0 + jnp.exp(-jnp.abs(x)))


def _log_sigmoid(x):
    return jnp.minimum(x, 0.0) - _log1p_exp_neg_abs(x)


def _softplus(x):
    return jnp.maximum(x, 0.0) + _log1p_exp_neg_abs(x)


def _silu(x):
    return x * jax.nn.sigmoid(x)


def _dot(a, b):
    return jnp.dot(a, b, preferred_element_type=F32)


def _dot_nt(a, b):
    return lax.dot_general(a, b, (((1,), (1,)), ((), ())), preferred_element_type=F32)


def _dot_tn(a, b):
    return lax.dot_general(a, b, (((0,), (0,)), ((), ())), preferred_element_type=F32)


def _split3(x):
    x1 = x.astype(BF16)
    r1 = x - x1.astype(F32)
    x2 = r1.astype(BF16)
    r2 = r1 - x2.astype(F32)
    return x1, x2, r2.astype(BF16)


def _cumsum_rows(tril, x):
    x1, x2, x3 = _split3(x)
    return _dot(tril, x1) + _dot(tril, x2) + _dot(tril, x3)


def _tril_ones(n):
    r = lax.broadcasted_iota(jnp.int32, (n, n), 0)
    c = lax.broadcasted_iota(jnp.int32, (n, n), 1)
    return r >= c


def _inproj_kernel(tm, tiles_per_seq, x_ref, nw_ref, wb_ref, wf_ref, cw_ref, cb_ref, ob_ref, of_ref, ubuf):
    i = pl.program_id(0)
    x = x_ref[...]
    h = x * lax.rsqrt(jnp.mean(x * x, axis=-1, keepdims=True) + NORM_EPS) * nw_ref[...]
    h = h.astype(BF16)
    step = 512
    conv_dim = BF_COLS["sxbc"][1]

    @pl.when(i % tiles_per_seq == 0)
    def _():
        ubuf[0:SUBLANES, :] = jnp.zeros((SUBLANES, conv_dim), F32)

    @pl.when(i % tiles_per_seq != 0)
    def _():
        ubuf[0:SUBLANES, :] = ubuf[tm:tm + SUBLANES, :]

    for c in range(0, conv_dim, step):
        ubuf[SUBLANES:SUBLANES + tm, c:c + step] = _dot(h, wb_ref[:, c:c + step])
    def conv_chunk(c):
        ext = ubuf[0:SUBLANES + tm, c:c + step]
        acc = cb_ref[:, c:c + step] + ext[SUBLANES:] * cw_ref[SSM_CONV - 1:SSM_CONV, c:c + step]
        for back in range(1, SSM_CONV):
            w = SSM_CONV - 1 - back
            acc = acc + pltpu.roll(ext, back, 0)[SUBLANES:] * cw_ref[w:w + 1, c:c + step]
        ob_ref[:, c:c + step] = _silu(acc).astype(ob_ref.dtype)

    conv_todo = list(range(0, conv_dim, step))
    for c in range(conv_dim, BF_WIDTH, step):
        ob_ref[:, c:c + step] = _dot(h, wb_ref[:, c:c + step]).astype(ob_ref.dtype)
        if conv_todo:
            conv_chunk(conv_todo.pop(0))
    of_ref[...] = _dot(h, wf_ref[...])


def _inproj(x2d, nw, wb, wf, cw, cb, layer, t_len, tm=256):
    n = x2d.shape[0]
    conv_dim = BF_COLS["sxbc"][1]
    assert BF_COLS["sxbc"][0] == 0 and t_len % tm == 0
    return pl.pallas_call(
        functools.partial(_inproj_kernel, tm, t_len // tm),
        grid=(n // tm,),
        in_specs=[
            pl.BlockSpec((tm, D_MODEL), lambda i: (i, 0)),
            pl.BlockSpec((None, 1, D_MODEL), lambda i: (layer, 0, 0)),
            pl.BlockSpec((None, D_MODEL, BF_WIDTH), lambda i: (layer, 0, 0), pipeline_mode=pl.Buffered(1)),
            pl.BlockSpec((None, D_MODEL, F_WIDTH), lambda i: (layer, 0, 0), pipeline_mode=pl.Buffered(1)),
            pl.BlockSpec((None, SSM_CONV, conv_dim), lambda i: (layer, 0, 0)),
            pl.BlockSpec((None, 1, conv_dim), lambda i: (layer, 0, 0)),
        ],
        out_specs=[
            pl.BlockSpec((tm, BF_WIDTH), lambda i: (i, 0)),
            pl.BlockSpec((tm, F_WIDTH), lambda i: (i, 0)),
        ],
        out_shape=[jax.ShapeDtypeStruct((n, BF_WIDTH), BF16), jax.ShapeDtypeStruct((n, F_WIDTH), F32)],
        scratch_shapes=[pltpu.VMEM((tm + 2 * SUBLANES, conv_dim), F32)],
        compiler_params=_cparams(("arbitrary",)),
        name="inproj",
    )(x2d, nw, wb, wf, cw, cb)


def _la_chunk_head(qh, kh, vh, bh, hsl, kc_ref, bc_ref, st_ref, h):
    C = LA_CHUNK
    row8 = lax.broadcasted_iota(jnp.int32, (SUBLANES, 1), 0)
    colc = lax.broadcasted_iota(jnp.int32, (SUBLANES, C), 1)
    b_last = bh[C - 1:C, :]
    st = st_ref[h]
    o = _dot_nt((qh * jnp.exp(bh)).astype(BF16), st.astype(BF16))
    tiles = []
    for blk in range(C // LA_SUB):
        r0 = blk * LA_SUB
        if blk == 0:
            halves = [jnp.zeros((SUBLANES, C), F32), jnp.zeros((SUBLANES, C), F32)]
        else:
            bs = bh[r0:r0 + 1, :]
            qi = (qh[r0:r0 + LA_SUB] * jnp.exp(bh[r0:r0 + LA_SUB] - bs)).astype(BF16)
            kj = kh[0:r0] * jnp.exp(bs - bh[0:r0])
            kj = jnp.concatenate([kj, jnp.zeros((C - r0, HEAD_W), F32)], axis=0).astype(BF16)
            s = _dot_nt(qi, kj)
            halves = [s[0:SUBLANES], s[SUBLANES:LA_SUB]]
        for j in range(LA_SUB):
            bj = bc_ref[r0 + j:r0 + j + 1, hsl]
            kj = kc_ref[r0 + j:r0 + j + 1, hsl]
            for half in range(2):
                if j >= SUBLANES * (half + 1):
                    continue
                rs = slice(r0 + SUBLANES * half, r0 + SUBLANES * (half + 1))
                d = bh[rs] - bj
                p = qh[rs] * kj
                if j >= SUBLANES * half:
                    m = row8 >= (j - SUBLANES * half)
                    p = jnp.where(m, p * jnp.exp(jnp.where(m, d, 0.0)), 0.0)
                else:
                    p = p * jnp.exp(d)
                r = jnp.sum(p, axis=-1, keepdims=True)
                halves[half] = jnp.where(colc == (r0 + j), r, halves[half])
        tiles += halves
    a = jnp.concatenate(tiles, axis=0).astype(BF16)
    o = o + _dot(a, vh)
    kte = (kh * jnp.exp(b_last - bh)).astype(BF16)
    st_ref[h] = st * jnp.exp(b_last) + _dot_tn(vh, kte)
    return o


def _la_mild_prep(qh, kh, vh, bh, causal):
    C = LA_CHUNK
    b_last = bh[C - 1:C, :]
    b_mid = bh[C // 2 - 1:C // 2, :]
    qm = qh * jnp.exp(bh - b_mid)
    ke = (kh * jnp.exp(b_mid - bh)).astype(BF16)
    a = jnp.where(causal, _dot_nt(qm.astype(BF16), ke), 0.0).astype(BF16)
    kte = (kh * jnp.exp(b_last - bh)).astype(BF16)
    return (qm * jnp.exp(b_mid)).astype(BF16), _dot(a, vh), _dot_tn(vh, kte), jnp.exp(b_last)


def _la_kernel(mode, nchunks, *refs):
    if mode == "gla":
        (q_ref, k_ref, v_ref, gate_ref, gl_ref, up_ref, gb_ref, nw_ref,
         o_ref, st_ref, q_s, k_s, b_s, o_s, kc_ref, bc_ref) = refs
    else:
        (q_ref, v_ref, gate_ref, f_ref, loglb_ref, log1m_ref, onem_ref, nw_ref,
         o_ref, st_ref, q_s, k_s, b_s, o_s, kc_ref, bc_ref) = refs

    @pl.when(pl.program_id(1) == 0)
    def _():
        st_ref[...] = jnp.zeros_like(st_ref)

    C = LA_CHUNK
    causal = _tril_ones(C)
    tril = causal.astype(BF16)

    if mode == "gla":
        q_s[...] = q_ref[...].astype(F32) * (GLA_DK ** -0.5)
        k_s[...] = k_ref[...].astype(F32)
        logit = _dot(gl_ref[...].astype(BF16), up_ref[...]) + gb_ref[...]
        g = _log_sigmoid(logit) * (1.0 / GLA_TEMP)
    else:
        q_s[...] = _silu(q_ref[...].astype(F32)) * (HEAD_W ** -0.5)
        f = f_ref[...]
        e = jnp.exp(-jnp.abs(f))
        u = loglb_ref[...]
        w = log1m_ref[...] + (jnp.minimum(f, 0.0) - jnp.log(1.0 + e))
        g = jnp.maximum(u, w) + _log1p_exp_neg_abs(u - w)
        k_s[...] = onem_ref[...] * (jnp.where(f >= 0.0, e, 1.0) / (1.0 + e))
    spread = jnp.zeros((1, MIX_W), F32)
    for c in range(nchunks):
        b = _cumsum_rows(tril, g[c * C:(c + 1) * C])
        b_s[c * C:(c + 1) * C, :] = b
        b_mid = b[C // 2 - 1:C // 2]
        spread = jnp.maximum(spread, jnp.maximum(b[0:1] - b_mid, b_mid - b[C - 1:C]))
    mild = jnp.max(spread) <= LA_MILD

    @pl.when(mild)
    def _():
        def body(gi, carry):
            base = pl.multiple_of(gi * (LA_GROUP * C), LA_GROUP * C)
            for h in range(HEADS):
                hsl = slice(h * HEAD_W, (h + 1) * HEAD_W)
                rows = [pl.ds(base + j * C, C) for j in range(LA_GROUP)]
                prep = [_la_mild_prep(q_s[rs, hsl], k_s[rs, hsl], v_ref[rs, hsl], b_s[rs, hsl], causal)
                        for rs in rows]
                st = st_ref[h]
                for rs, (qe, o_intra, inc, dec) in zip(rows, prep):
                    o_s[rs, hsl] = o_intra + _dot_nt(qe, st.astype(BF16))
                    st = st * dec + inc
                st_ref[h] = st
            return carry
        lax.fori_loop(0, nchunks // LA_GROUP, body, 0)

    @pl.when(jnp.logical_not(mild))
    def _():
        def body(c, carry):
            rs = pl.ds(pl.multiple_of(c * C, C), C)
            kc_ref[...] = k_s[rs, :]
            bc_ref[...] = b_s[rs, :]
            for h in range(HEADS):
                hsl = slice(h * HEAD_W, (h + 1) * HEAD_W)
                o_s[rs, hsl] = _la_chunk_head(q_s[rs, hsl], k_s[rs, hsl], v_ref[rs, hsl], b_s[rs, hsl],
                                              hsl, kc_ref, bc_ref, st_ref, h)
            return carry
        lax.fori_loop(0, nchunks, body, 0)

    for h in range(HEADS):
        hsl = slice(h * HEAD_W, (h + 1) * HEAD_W)
        o = o_s[:, hsl]
        y = o * lax.rsqrt(jnp.mean(o * o, axis=-1, keepdims=True) + NORM_EPS) * nw_ref[:, hsl]
        o_ref[:, hsl] = (y * _silu(gate_ref[:, hsl].astype(F32))).astype(o_ref.dtype)


def _col_spec(tb, tblocks, width, off):
    assert off % width == 0
    cb = off // width
    return pl.BlockSpec((tb, width), lambda b, t: (b * tblocks + t, cb))


def _row_param(width, layer):
    return pl.BlockSpec((None, 1, width), lambda b, t: (layer, 0, 0))


def _lin_attn(mode, pb, pf, params, layer, bsz, t_len, tb=512):
    n = bsz * t_len
    tblocks = t_len // tb
    sp = functools.partial(_col_spec, tb, tblocks)
    if mode == "gla":
        up, gb, nw = params
        ins = [pb, pb, pb, pb, pf, up, gb, nw]
        specs = [sp(MIX_W, BF_COLS["gq"][0]), sp(MIX_W, BF_COLS["gk"][0]), sp(MIX_W, BF_COLS["gv"][0]),
                 sp(MIX_W, BF_COLS["gr"][0]), sp(LANES, F_COLS["glow"][0]),
                 pl.BlockSpec((None, LANES, MIX_W), lambda b, t: (layer, 0, 0)),
                 _row_param(MIX_W, layer), _row_param(MIX_W, layer)]
    else:
        loglb, log1m, onem, nw = params
        ins = [pb, pb, pb, pf, loglb, log1m, onem, nw]
        specs = [sp(MIX_W, BF_COLS["hq"][0]), sp(MIX_W, BF_COLS["hi"][0]), sp(MIX_W, BF_COLS["hg"][0]),
                 sp(MIX_W, F_COLS["hf"][0]),
                 _row_param(MIX_W, layer), _row_param(MIX_W, layer), _row_param(MIX_W, layer),
                 _row_param(MIX_W, layer)]
    return pl.pallas_call(
        functools.partial(_la_kernel, mode, tb // LA_CHUNK),
        grid=(bsz, tblocks),
        in_specs=specs,
        out_specs=pl.BlockSpec((tb, MIX_W), lambda b, t: (b * tblocks + t, 0)),
        out_shape=jax.ShapeDtypeStruct((n, MIX_W), BF16),
        scratch_shapes=([pltpu.VMEM((HEADS, HEAD_W, HEAD_W), F32)] + [pltpu.VMEM((tb, MIX_W), F32)] * 4
                        + [pltpu.VMEM((LA_CHUNK, MIX_W), F32)] * 2),
        compiler_params=_cparams(("parallel", "arbitrary")),
        name="lin_attn_" + mode,
    )(*ins)


SSD_B_PARTS = 3
SSD_DT_PARTS = 2
SSD_SEL_COLS = SSM_HEADS * LANES + (SSM_HEADS // 2) * LANES


def _ssd_select():
    k = np.arange(LANES)[:, None]
    n = np.arange(SSD_SEL_COLS)[None, :]
    nb = SSM_HEADS * LANES
    h = n // LANES
    sel_b = (n < nb) & (k < SSD_B_PARTS * SSM_HEADS) & (k % SSM_HEADS == h)
    hh = 2 * ((n - nb) // LANES) + ((n - nb) % LANES >= SSM_P)
    kd = k - SSD_B_PARTS * SSM_HEADS
    sel_d = (n >= nb) & (kd >= 0) & (kd < SSD_DT_PARTS * SSM_HEADS) & (kd % SSM_HEADS == hh)
    return jnp.asarray((sel_b | sel_d).astype(np.float32), dtype=BF16)


def _ssd_kernel(nchunks, xbc_ref, z_ref, dt_ref, sel_ref, dtb_ref, alog_ref, dvec_ref, nw_ref, o_ref, st_ref):
    @pl.when(pl.program_id(1) == 0)
    def _():
        st_ref[...] = jnp.zeros_like(st_ref)

    C = SSD_CHUNK
    causal = _tril_ones(C)
    tril = causal.astype(BF16)
    lane = lax.broadcasted_iota(jnp.int32, (C, LANES), 1)
    row = lax.broadcasted_iota(jnp.int32, (C, LANES), 0)
    lo_lane = lane < SSM_P
    lo_row = row < SSM_P
    head_lane = lane < SSM_HEADS
    a_neg = -jnp.exp(alog_ref[...])
    d_inner = SSM_HEADS * SSM_P
    gn = SSM_GROUPS * SSM_N
    hpg = SSM_HEADS // SSM_GROUPS

    def body(c, carry):
        rs = pl.ds(pl.multiple_of(c * C, C), C)
        dt = _softplus(dt_ref[rs, :] + dtb_ref[...])
        b_col = _cumsum_rows(tril, dt * a_neg)
        b_row = b_col.T
        dt_row = dt.T
        e_last = jnp.exp(b_col[C - 1:C, :])
        d1 = dt.astype(BF16)
        parts = list(_split3(b_col)) + [d1, (dt - d1.astype(F32)).astype(BF16)]
        packed = jnp.zeros((C, LANES), F32)
        for idx, part in enumerate(parts):
            v = jnp.where(head_lane, part.astype(F32), 0.0)
            packed = packed + (pltpu.roll(v, SSM_HEADS * idx, 1) if idx else v)
        bc = _dot(packed.astype(BF16), sel_ref[...])
        for g in range(SSM_GROUPS):
            bg = xbc_ref[rs, d_inner + g * SSM_N:d_inner + (g + 1) * SSM_N]
            cg = xbc_ref[rs, d_inner + gn + g * SSM_N:d_inner + gn + (g + 1) * SSM_N]
            cbm = _dot_nt(cg, bg)
            ys = []
            for pp in range(hpg // 2):
                p = g * (hpg // 2) + pp
                ha, hb = 2 * p, 2 * p + 1
                psl = slice(p * LANES, (p + 1) * LANES)
                x2b = xbc_ref[rs, psl]
                x2 = x2b.astype(F32)
                b_a = bc[:, ha * LANES:(ha + 1) * LANES]
                b_b = bc[:, hb * LANES:(hb + 1) * LANES]
                dt_sel = bc[:, (SSM_HEADS + p) * LANES:(SSM_HEADS + p + 1) * LANES]

                def wmat(h, b_i):
                    seg = b_i - b_row[h:h + 1, :]
                    dec = jnp.where(causal, jnp.exp(jnp.where(causal, seg, 0.0)), 0.0)
                    return (dec * cbm * dt_row[h:h + 1, :]).astype(BF16)

                y2 = jnp.where(lo_lane, _dot(wmat(ha, b_a), x2b), _dot(wmat(hb, b_b), x2b))
                stp = st_ref[p]
                b_sel = jnp.where(lo_lane, b_a, b_b)
                y2 = y2 + _dot_nt(cg, stp.astype(BF16)) * jnp.exp(b_sel)
                wsel = jnp.exp(b_sel[C - 1:C, :] - b_sel) * dt_sel
                upd = _dot_tn((x2 * wsel).astype(BF16), bg)
                dsel = jnp.where(lo_row, e_last[:, ha:ha + 1], e_last[:, hb:hb + 1])
                st_ref[p] = stp * dsel + upd
                y2 = y2 + dvec_ref[:, psl] * x2
                y2 = y2 * _silu(z_ref[rs, psl].astype(F32))
                ys.append(y2)
            yg = jnp.concatenate(ys, axis=-1)
            var = jnp.mean(yg * yg, axis=-1, keepdims=True)
            gsl = slice(g * 2 * LANES, (g + 1) * 2 * LANES)
            o_ref[rs, gsl] = (yg * lax.rsqrt(var + NORM_EPS) * nw_ref[:, gsl]).astype(o_ref.dtype)
        return carry

    lax.fori_loop(0, nchunks, body, 0)


def _ssd(pb, pf, sel, dtb, alog, dvec, nw, layer, bsz, t_len, tb=256):
    n = bsz * t_len
    tblocks = t_len // tb
    sp = functools.partial(_col_spec, tb, tblocks)
    conv_dim = SSM_HEADS * SSM_P + 2 * SSM_GROUPS * SSM_N
    d_inner = SSM_HEADS * SSM_P
    return pl.pallas_call(
        functools.partial(_ssd_kernel, tb // SSD_CHUNK),
        grid=(bsz, tblocks),
        in_specs=[sp(conv_dim, BF_COLS["sxbc"][0]), sp(d_inner, BF_COLS["sz"][0]), sp(LANES, F_COLS["sdt"][0]),
                  pl.BlockSpec((LANES, SSD_SEL_COLS), lambda b, t: (0, 0)),
                  _row_param(LANES, layer), _row_param(LANES, layer),
                  _row_param(d_inner, layer), _row_param(d_inner, layer)],
        out_specs=pl.BlockSpec((tb, d_inner), lambda b, t: (b * tblocks + t, 0)),
        out_shape=jax.ShapeDtypeStruct((n, d_inner), BF16),
        scratch_shapes=[pltpu.VMEM((SSM_HEADS // 2, 2 * SSM_P, SSM_N), F32)],
        compiler_params=_cparams(("parallel", "arbitrary")),
        name="ssd",
    )(pb, pb, pf, sel, dtb, alog, dvec, nw)


def _merge_kernel(nsub, yg_ref, yh_ref, ys_ref, gates_ref, x_ref, w1_ref, w2_ref, w3_ref, wo_ref,
                  n2_ref, rwh_ref, rwl_ref, rb_ref, xo_ref, h2_ref, ri_ref, cnt_ref, run_ref):
    @pl.when(pl.program_id(0) == 0)
    def _():
        run_ref[...] = jnp.zeros_like(run_ref)

    sub = x_ref.shape[0] // nsub

    def matmul_stages(rs):
        def branch(k, y_ref, w_ref):
            g = jax.nn.sigmoid(gates_ref[rs, k * D_MODEL:(k + 1) * D_MODEL].astype(F32))
            return g * _dot(y_ref[rs, :], w_ref[...])
        m = branch(0, yg_ref, w1_ref)
        yield
        m = m + branch(1, yh_ref, w2_ref)
        yield
        m = m + branch(2, ys_ref, w3_ref)
        yield
        xo_ref[rs, :] = x_ref[rs, :] + _dot(m.astype(BF16), wo_ref[...])
        yield

    routing = iter(())
    for s in range(nsub):
        rs = slice(s * sub, (s + 1) * sub)
        for _ in matmul_stages(rs):
            next(routing, None)
        for _ in routing:
            pass
        routing = _route(xo_ref, n2_ref, rwh_ref, rwl_ref, rb_ref, h2_ref, ri_ref, run_ref, rs)
    for _ in routing:
        pass
    cnt_ref[...] = run_ref[...]


def _route(xo_ref, n2_ref, rwh_ref, rwl_ref, rb_ref, h2_ref, ri_ref, run_ref, rs):
    xn = xo_ref[rs, :]
    h2 = xn * lax.rsqrt(jnp.mean(xn * xn, axis=-1, keepdims=True) + NORM_EPS) * n2_ref[...]
    h2_ref[rs, :] = h2
    yield
    hi = h2.astype(BF16)
    lo = (h2 - hi.astype(F32)).astype(BF16)
    logits = _dot(hi, rwh_ref[...]) + _dot(lo, rwh_ref[...]) + _dot(hi, rwl_ref[...]) + rb_ref[...]
    yield

    lane = lax.broadcasted_iota(jnp.int32, logits.shape, 1)
    lanef = lane.astype(F32)
    neg = jnp.float32(-1e30)
    big = jnp.float32(1e9)
    gmask = lane < MOE_GROUPS
    gl = jnp.where(gmask, logits, neg)
    gexp = jnp.where(gmask, jnp.exp(gl - jnp.max(gl, axis=-1, keepdims=True)), 0.0)
    gprob = gexp / jnp.sum(gexp, axis=-1, keepdims=True)
    g_p = jnp.max(gprob, axis=-1, keepdims=True)
    g_idx = jnp.min(jnp.where(gmask & (gprob == g_p), lanef, big), axis=-1, keepdims=True)
    yield
    lo_l = MOE_GROUPS + EXPERTS_PER_GROUP * g_idx
    emask = (lanef >= lo_l) & (lanef < lo_l + EXPERTS_PER_GROUP)
    el = jnp.where(emask, logits, neg)
    eexp = jnp.where(emask, jnp.exp(el - jnp.max(el, axis=-1, keepdims=True)), 0.0)
    eprob = eexp / jnp.sum(eexp, axis=-1, keepdims=True)
    p1 = jnp.max(jnp.where(emask, eprob, -1.0), axis=-1, keepdims=True)
    i1 = jnp.min(jnp.where(emask & (eprob == p1), lanef, big), axis=-1, keepdims=True)
    rest = emask & (lanef != i1)
    p2 = jnp.max(jnp.where(rest, eprob, -1.0), axis=-1, keepdims=True)
    i2 = jnp.min(jnp.where(rest & (eprob == p2), lanef, big), axis=-1, keepdims=True)
    den = p1 + p2
    w_a = g_p * (p1 / den)
    w_b = g_p * (p2 / den)
    yield

    nrow = logits.shape[0]
    r_i = lax.broadcasted_iota(jnp.int32, (nrow, nrow), 0)
    c_i = lax.broadcasted_iota(jnp.int32, (nrow, nrow), 1)
    before = (r_i > c_i).astype(BF16)
    oh1 = (lanef == i1).astype(F32)
    oh2 = (lanef == i2).astype(F32)
    run = run_ref[...]
    c1 = jnp.sum(oh1, axis=0, keepdims=True)
    pos1 = jnp.sum(oh1 * (run + _dot(before, oh1.astype(BF16))), axis=-1, keepdims=True)
    pos2 = jnp.sum(oh2 * (run + c1 + _dot(before, oh2.astype(BF16))), axis=-1, keepdims=True)
    run_ref[...] = run + c1 + jnp.sum(oh2, axis=0, keepdims=True)

    out = jnp.zeros_like(logits)
    for k, val in enumerate((i1 - MOE_GROUPS, i2 - MOE_GROUPS, w_a, w_b, pos1, pos2)):
        out = jnp.where(lane == k, val, out)
    ri_ref[rs, :] = out


def _merge(yg, yh, ys, pb, x2d, w1, w2, w3, wo, n2, rwh, rwl, rb, layer, tm=512, nsub=2):
    n = x2d.shape[0]
    gates_blk = BF_COLS["gates"][0] // BF_COLS["gates"][1]

    def wspec(k):
        return pl.BlockSpec((None, k, D_MODEL), lambda i: (layer, 0, 0))

    return pl.pallas_call(
        functools.partial(_merge_kernel, nsub),
        grid=(n // tm,),
        in_specs=[pl.BlockSpec((tm, MIX_W), lambda i: (i, 0)),
                  pl.BlockSpec((tm, MIX_W), lambda i: (i, 0)),
                  pl.BlockSpec((tm, D_MODEL), lambda i: (i, 0)),
                  pl.BlockSpec((tm, 3 * D_MODEL), lambda i: (i, gates_blk)),
                  pl.BlockSpec((tm, D_MODEL), lambda i: (i, 0)),
                  wspec(MIX_W), wspec(MIX_W), wspec(D_MODEL), wspec(D_MODEL),
                  pl.BlockSpec((None, 1, D_MODEL), lambda i: (layer, 0, 0)),
                  pl.BlockSpec((None, D_MODEL, LANES), lambda i: (layer, 0, 0)),
                  pl.BlockSpec((None, D_MODEL, LANES), lambda i: (layer, 0, 0)),
                  pl.BlockSpec((None, 1, LANES), lambda i: (layer, 0, 0))],
        out_specs=[pl.BlockSpec((tm, D_MODEL), lambda i: (i, 0)),
                   pl.BlockSpec((tm, D_MODEL), lambda i: (i, 0)),
                   pl.BlockSpec((tm, LANES), lambda i: (i, 0)),
                   pl.BlockSpec((1, LANES), lambda i: (0, 0))],
        out_shape=[jax.ShapeDtypeStruct((n, D_MODEL), F32),
                   jax.ShapeDtypeStruct((n, D_MODEL), F32),
                   jax.ShapeDtypeStruct((n, LANES), F32),
                   jax.ShapeDtypeStruct((1, LANES), F32)],
        scratch_shapes=[pltpu.VMEM((1, LANES), F32)],
        compiler_params=_cparams(("arbitrary",)),
        name="merge_router",
    )(yg, yh, ys, pb, x2d, w1, w2, w3, wo, n2, rwh, rwl, rb)


def _dispatch_kernel(tp, nsteps, blk, idx_hbm, pstart_ref, pend_ref, h_ref, xb_out, idx_smem, zbuf, isem, rsem,
                     zsem):
    i = pl.program_id(0)
    slot = i % 2

    def idx_copy(step, s):
        return pltpu.make_async_copy(idx_hbm.at[step], idx_smem.at[s], isem.at[s])

    def row_copy(r, dst):
        return pltpu.make_async_copy(h_ref.at[pl.ds(r, 1)], xb_out.at[pl.ds(dst, 1)], rsem)

    def zero_copy(start):
        if not isinstance(start, int):
            start = pl.multiple_of(start, blk)
        return pltpu.make_async_copy(zbuf, xb_out.at[pl.ds(start, blk)], zsem)

    @pl.when(i == 0)
    def _():
        idx_copy(0, 0).start()
        if nsteps > 1:
            idx_copy(1, 1).start()
        zbuf[...] = jnp.zeros_like(zbuf)
        total = xb_out.shape[0]
        todo = [(pend_ref[e] > pstart_ref[e], pend_ref[e] - blk) for e in range(N_EXPERTS)]
        todo += [(start >= pend_ref[N_EXPERTS - 1], start) for start in range(total - N_EXPERTS * blk, total, blk)]
        for cond, start in todo:
            @pl.when(cond)
            def _():
                zero_copy(start).start()
        for cond, start in todo:
            @pl.when(cond)
            def _():
                zero_copy(start).wait()

    idx_copy(i, slot).wait()

    for r in range(tp):
        for k in range(2):
            row_copy(r, idx_smem[slot, k * tp + r]).start()

    @pl.when(i + 2 < nsteps)
    def _():
        idx_copy(i + 2, slot).start()

    def drain(r, carry):
        for k in range(2):
            row_copy(0, 0).wait()
        return carry

    lax.fori_loop(0, tp, drain, 0, unroll=8)


def _dispatch(dest_tiles, pstart, pend, h2, rows, tp, blk):
    n = h2.shape[0]
    return pl.pallas_call(
        functools.partial(_dispatch_kernel, tp, n // tp, blk),
        grid=(n // tp,),
        in_specs=[pl.BlockSpec(memory_space=pl.ANY),
                  pl.BlockSpec(memory_space=pltpu.SMEM),
                  pl.BlockSpec(memory_space=pltpu.SMEM),
                  pl.BlockSpec((tp, D_MODEL), lambda i: (i, 0))],
        out_specs=pl.BlockSpec(memory_space=pl.ANY),
        out_shape=jax.ShapeDtypeStruct((rows, D_MODEL), F32),
        scratch_shapes=[pltpu.SMEM((2, 2 * tp), jnp.int32),
                        pltpu.VMEM((blk, D_MODEL), F32),
                        pltpu.SemaphoreType.DMA((2,)),
                        pltpu.SemaphoreType.DMA(()),
                        pltpu.SemaphoreType.DMA(())],
        compiler_params=_cparams(("arbitrary",)),
        name="moe_dispatch",
    )(dest_tiles, pstart, pend, h2)


def _ffn_kernel(be_ref, nb_ref, x_ref, wg_ref, wu_ref, wd_ref, o_ref, wg_s, wu_s, wd_s):
    i = pl.program_id(0)
    used = i < nb_ref[0]

    @pl.when(used & ((i == 0) | (be_ref[i] != be_ref[jnp.maximum(i - 1, 0)])))
    def _():
        wg_s[...] = wg_ref[...].astype(BF16)
        wu_s[...] = wu_ref[...].astype(BF16)
        wd_s[...] = wd_ref[...].astype(BF16)

    @pl.when(used)
    def _():
        x = x_ref[...].astype(BF16)
        hid = _silu(_dot(x, wg_s[...])) * _dot(x, wu_s[...])
        o_ref[...] = _dot(hid.astype(BF16), wd_s[...])

    @pl.when(jnp.logical_not(used))
    def _():
        o_ref[...] = jnp.zeros_like(o_ref)


def _ffn(block_e, nblk_used, xb, wg, wu, wd, layer, blk):
    rows = xb.shape[0]

    def x_map(i, be, nb):
        return (jnp.minimum(i, nb[0] - 1), 0)

    def w_map(i, be, nb):
        return (layer, be[jnp.minimum(i, nb[0] - 1)], 0, 0)

    grid_spec = pltpu.PrefetchScalarGridSpec(
        num_scalar_prefetch=2,
        grid=(rows // blk,),
        in_specs=[pl.BlockSpec((blk, D_MODEL), x_map),
                  pl.BlockSpec((None, None, D_MODEL, MOE_FF), w_map),
                  pl.BlockSpec((None, None, D_MODEL, MOE_FF), w_map),
                  pl.BlockSpec((None, None, MOE_FF, D_MODEL), w_map)],
        out_specs=pl.BlockSpec((blk, D_MODEL), lambda i, be, nb: (i, 0)),
        scratch_shapes=[pltpu.VMEM((D_MODEL, MOE_FF), BF16),
                        pltpu.VMEM((D_MODEL, MOE_FF), BF16),
                        pltpu.VMEM((MOE_FF, D_MODEL), BF16)],
    )
    return pl.pallas_call(
        _ffn_kernel,
        grid_spec=grid_spec,
        out_shape=jax.ShapeDtypeStruct((rows, D_MODEL), F32),
        compiler_params=_cparams(("arbitrary",)),
        name="moe_ffn",
    )(block_e, nblk_used, xb, wg, wu, wd)


def _combine_kernel(tc, nsteps, final, idx_hbm, yb_hbm, x_ref, w_ref, fw_ref, o_ref, idx_smem, ybuf, isem, rsem):
    i = pl.program_id(0)
    slot = i % 2

    def idx_copy(step, s):
        return pltpu.make_async_copy(idx_hbm.at[step], idx_smem.at[s], isem.at[s])

    def row_copy(src, r, s):
        return pltpu.make_async_copy(yb_hbm.at[pl.ds(src, 1)], ybuf.at[s, pl.ds(r, 1)], rsem.at[s])

    def issue_rows(s):
        for r in range(2 * tc):
            row_copy(idx_smem[s, r], r, s).start()

    @pl.when(i == 0)
    def _():
        idx_copy(0, 0).start()
        idx_copy(0, 0).wait()
        issue_rows(0)
        if nsteps > 1:
            idx_copy(1, 1).start()

    @pl.when(i + 1 < nsteps)
    def _():
        idx_copy(i + 1, 1 - slot).wait()
        issue_rows(1 - slot)

    @pl.when(i + 2 < nsteps)
    def _():
        idx_copy(i + 2, slot).start()

    def drain(r, carry):
        row_copy(0, 0, slot).wait()
        return carry

    lax.fori_loop(0, 2 * tc, drain, 0, unroll=8)
    w = w_ref[...]
    out = x_ref[...] + w[:, 2:3] * ybuf[slot, 0:tc, :] + w[:, 3:4] * ybuf[slot, tc:2 * tc, :]
    if final:
        out = out * lax.rsqrt(jnp.mean(out * out, axis=-1, keepdims=True) + NORM_EPS) * fw_ref[...]
    o_ref[...] = out


def _combine(dest_tiles, yb, x2d, ri, fw, final, tc):
    n = x2d.shape[0]
    return pl.pallas_call(
        functools.partial(_combine_kernel, tc, n // tc, final),
        grid=(n // tc,),
        in_specs=[pl.BlockSpec(memory_space=pl.ANY),
                  pl.BlockSpec(memory_space=pl.ANY),
                  pl.BlockSpec((tc, D_MODEL), lambda i: (i, 0)),
                  pl.BlockSpec((tc, LANES), lambda i: (i, 0)),
                  pl.BlockSpec((1, D_MODEL), lambda i: (0, 0))],
        out_specs=pl.BlockSpec((tc, D_MODEL), lambda i: (i, 0)),
        out_shape=jax.ShapeDtypeStruct((n, D_MODEL), F32),
        scratch_shapes=[pltpu.SMEM((2, 2 * tc), jnp.int32),
                        pltpu.VMEM((2, 2 * tc, D_MODEL), F32),
                        pltpu.SemaphoreType.DMA((2,)),
                        pltpu.SemaphoreType.DMA((2,))],
        compiler_params=_cparams(("arbitrary",)),
        name="moe_combine",
    )(dest_tiles, yb, x2d, ri, fw)


def _route_slots(ri, cnt, blk):
    n = ri.shape[0]
    e = ri[:, 0:2].astype(jnp.int32)
    pos = ri[:, 4:6].astype(jnp.int32)
    counts = cnt[0, MOE_GROUPS:MOE_GROUPS + N_EXPERTS].astype(jnp.int32)
    padded = (counts + blk - 1) // blk * blk
    pend = jnp.cumsum(padded)
    pstart = pend - padded
    experts = jnp.arange(N_EXPERTS, dtype=jnp.int32)
    dest = (jnp.sum(jnp.where(e[:, :, None] == experts, pstart, 0), axis=-1) + pos).astype(jnp.int32)
    rows = 2 * n + N_EXPERTS * blk
    nblk = rows // blk
    block_start = jnp.arange(nblk, dtype=jnp.int32) * blk
    block_e = jnp.minimum(jnp.sum((block_start[:, None] >= pend[None, :]).astype(jnp.int32), axis=1),
                          N_EXPERTS - 1).astype(jnp.int32)
    nblk_used = (pend[-1] // blk).astype(jnp.int32).reshape(1)
    return dest, block_e, nblk_used, pstart.astype(jnp.int32), pend.astype(jnp.int32), rows


def _tile_slots(dest, tile):
    n = dest.shape[0]
    return dest.reshape(n // tile, tile, 2).transpose(0, 2, 1).reshape(n // tile, 2 * tile)


def _pad_heads(w, heads, width):
    lead = w.shape[:-1]
    w = w.reshape(lead + (heads, width))
    w = jnp.pad(w, [(0, 0)] * len(lead) + [(0, 0), (0, HEAD_W - width)])
    return w.reshape(lead + (heads * HEAD_W,))


def _pad_last(w, width):
    return jnp.pad(w, [(0, 0)] * (w.ndim - 1) + [(0, width - w.shape[-1])])


def _prep_inproj(w_in):
    gla_kw, gla_vw, hw = HEADS * GLA_DK, MIX_W, MIX_W
    d_inner = SSM_HEADS * SSM_P
    conv_dim = d_inner + 2 * SSM_GROUPS * SSM_N
    sizes = (gla_kw, gla_kw, gla_vw, GLA_RANK, gla_vw, hw, hw, hw, hw, d_inner, conv_dim, SSM_HEADS, 3 * D_MODEL)
    offs = [0]
    for s in sizes:
        offs.append(offs[-1] + s)
    seg = {name: w_in[..., offs[i]:offs[i + 1]] for i, name in enumerate(
        ("gq", "gk", "gv", "glow", "gr", "hq", "hf", "hi", "hg", "sz", "sxbc", "sdt", "gates"))}
    seg["gq"] = _pad_heads(seg["gq"], HEADS, GLA_DK)
    seg["gk"] = _pad_heads(seg["gk"], HEADS, GLA_DK)
    seg["glow"] = _pad_last(seg["glow"], LANES)
    seg["sdt"] = _pad_last(seg["sdt"], LANES)
    wb = jnp.concatenate([seg[k] for k in sorted(BF_COLS, key=lambda k: BF_COLS[k][0])], axis=-1).astype(BF16)
    wf = jnp.concatenate([seg[k] for k in sorted(F_COLS, key=lambda k: F_COLS[k][0])], axis=-1).astype(BF16)
    return wb, wf


def kernel(x, norm1_w, w_in, gla_gk_up, gla_gk_bias, gla_norm_w, hgrn_lb_logits, hgrn_norm_w,
           ssm_conv_w, ssm_conv_b, ssm_dt_bias, ssm_a_log, ssm_d, ssm_norm_w,
           w_br_gla, w_br_hgrn, w_br_ssm, w_out, norm2_w,
           router_group_w, router_group_b, router_expert_w, router_expert_b,
           moe_w_gate, moe_w_up, moe_w_down, final_norm_w,
           *, moe_blk=512, disp_tile=512, comb_tile=256, la_tb=512, ssd_tb=256, row_tile=256, merge_tile=512):
    bsz, t_len, d = x.shape
    n = bsz * t_len
    depth = w_in.shape[0]
    x2d = x.reshape(n, d).astype(F32)

    wb, wf = _prep_inproj(w_in)
    row = lambda a: a.astype(F32)[:, None, :]
    norm1 = row(norm1_w)
    norm2 = row(norm2_w)
    up = _pad_heads(jnp.pad(gla_gk_up, ((0, 0), (0, LANES - GLA_RANK), (0, 0))), HEADS, GLA_DK).astype(BF16)
    gb = row(_pad_heads(gla_gk_bias, HEADS, GLA_DK))
    gnw = row(gla_norm_w)
    lower = jnp.cumsum(jax.nn.softmax(hgrn_lb_logits.astype(F32), axis=0), axis=0)
    lb = lower - lower[0:1]
    loglb, log1m, onem = row(jnp.log(lb)), row(jnp.log1p(-lb)), row(1.0 - lb)
    hnw = row(hgrn_norm_w)
    cw = ssm_conv_w.astype(F32)
    cb = row(ssm_conv_b)
    dtb = row(_pad_last(ssm_dt_bias, LANES))
    alog = row(_pad_last(ssm_a_log, LANES))
    dvec = row(jnp.repeat(ssm_d, SSM_P, axis=-1))
    snw = row(ssm_norm_w)
    sel = _ssd_select()
    w1, w2, w3, wo = (w.astype(BF16) for w in (w_br_gla, w_br_hgrn, w_br_ssm, w_out))
    rw = _pad_last(jnp.concatenate([router_group_w, router_expert_w], axis=-1).astype(F32), LANES)
    rwh = rw.astype(BF16)
    rwl = (rw - rwh.astype(F32)).astype(BF16)
    rb = row(_pad_last(jnp.concatenate([router_group_b, router_expert_b], axis=-1), LANES))
    fw =final_norm_w.astype(F32)[None, :]

    for l in range(depth):
        pb, pf = _inproj(x2d, norm1, wb, wf, cw, cb, l, t_len, tm=row_tile)
        yg = _lin_attn("gla", pb, pf, (up, gb, gnw), l, bsz, t_len, tb=la_tb)
        yh = _lin_attn("hgrn", pb, pf, (loglb, log1m, onem, hnw), l, bsz, t_len, tb=la_tb)
        ys = _ssd(pb, pf, sel, dtb, alog, dvec, snw, l, bsz, t_len, tb=ssd_tb)
        x2d, h2, ri, cnt = _merge(yg, yh, ys, pb, x2d, w1, w2, w3, wo, norm2, rwh, rwl, rb, l, tm=merge_tile)
        dest, block_e, nblk_used, pstart, pend, rows = _route_slots(ri, cnt, moe_blk)
        xb = _dispatch(_tile_slots(dest, disp_tile), pstart, pend, h2, rows, disp_tile, moe_blk)
        yb = _ffn(block_e, nblk_used, xb, moe_w_gate, moe_w_up, moe_w_down, l, moe_blk)
        x2d = _combine(_tile_slots(dest, comb_tile), yb, x2d, ri, fw, l == depth - 1, comb_tile)
    return x2d.reshape(bsz, t_len, d)
```

```python
import functools

import jax
import jax.numpy as jnp
import numpy as np
from jax import lax
from jax.experimental import pallas as pl
from jax.experimental.pallas import tpu as pltpu

F32 = jnp.float32
BF16 = jnp.bfloat16

D_MODEL = 1024
DEPTH = 4
NORM_EPS = 1e-6
HEADS = 4
HEAD_W = 128
MIX_W = HEADS * HEAD_W
GLA_DK = 64
GLA_RANK = 16
GLA_TEMP = 16.0
SSM_HEADS = 16
SSM_P = 64
SSM_GROUPS = 4
SSM_N = 128
SSM_CONV = 4
N_EXPERTS = 32
EXPERTS_PER_GROUP = 8
MOE_GROUPS = 4
MOE_FF = 512

LANES = 128
SUBLANES = 8

LA_CHUNK = 64
LA_SUB = 16
LA_GROUP = 8
LA_MILD = 60.0
SSD_CHUNK = 128

BF_COLS = dict(sxbc=(0, 2048), sz=(2048, 1024), gates=(3072, 3072), gv=(6144, 512), gr=(6656, 512),
               hq=(7168, 512), hi=(7680, 512), hg=(8192, 512), gq=(8704, 512), gk=(9216, 512))
BF_WIDTH = 9728
F_COLS = dict(hf=(0, 512), glow=(512, 128), sdt=(640, 128))
F_WIDTH = 768

VMEM_LIMIT = 56 * 1024 * 1024

INPROJ_TILE = 256
LA_TILE = 512
SSD_TILE = 512
MERGE_TILE = 512
MERGE_SUBTILES = 2
MOE_BLOCK = 512
DISPATCH_TILE = 512
COMBINE_TILE = 256


def _cparams(sem):
    return pltpu.CompilerParams(dimension_semantics=sem, vmem_limit_bytes=VMEM_LIMIT)


def _log1p_exp_neg_abs(x):
    return jnp.log(1.0 + jnp.exp(-jnp.abs(x)))


def _log_sigmoid(x):
    return jnp.minimum(x, 0.0) - _log1p_exp_neg_abs(x)


def _softplus(x):
    return jnp.maximum(x, 0.0) + _log1p_exp_neg_abs(x)


def _silu(x):
    return x * jax.nn.sigmoid(x)


def _dot(a, b):
    return jnp.dot(a, b, preferred_element_type=F32)


def _dot_nt(a, b):
    return lax.dot_general(a, b, (((1,), (1,)), ((), ())), preferred_element_type=F32)


def _dot_tn(a, b):
    return lax.dot_general(a, b, (((0,), (0,)), ((), ())), preferred_element_type=F32)


def _split3(x):
    x1 = x.astype(BF16)
    r1 = x - x1.astype(F32)
    x2 = r1.astype(BF16)
    r2 = r1 - x2.astype(F32)
    return x1, x2, r2.astype(BF16)


def _cumsum_rows(tril, x):
    x1, x2, x3 = _split3(x)
    return _dot(tril, x1) + _dot(tril, x2) + _dot(tril, x3)


def _tril_ones(n):
    r = lax.broadcasted_iota(jnp.int32, (n, n), 0)
    c = lax.broadcasted_iota(jnp.int32, (n, n), 1)
    return r >= c


def _inproj_kernel(tm, tiles_per_seq, x_ref, nw_ref, wb_ref, wf_ref, cw_ref, cb_ref, ob_ref, of_ref, ubuf):
    i = pl.program_id(0)
    x = x_ref[...]
    h = x * lax.rsqrt(jnp.mean(x * x, axis=-1, keepdims=True) + NORM_EPS) * nw_ref[...]
    h = h.astype(BF16)
    step = 512
    conv_dim = BF_COLS["sxbc"][1]

    @pl.when(i % tiles_per_seq == 0)
    def _():
        ubuf[0:SUBLANES, :] = jnp.zeros((SUBLANES, conv_dim), F32)

    @pl.when(i % tiles_per_seq != 0)
    def _():
        ubuf[0:SUBLANES, :] = ubuf[tm:tm + SUBLANES, :]

    for c in range(0, conv_dim, step):
        ubuf[SUBLANES:SUBLANES + tm, c:c + step] = _dot(h, wb_ref[:, c:c + step])
    for c in range(0, conv_dim, step):
        ext = ubuf[0:SUBLANES + tm, c:c + step]
        acc = cb_ref[:, c:c + step] + ext[SUBLANES:] * cw_ref[SSM_CONV - 1:SSM_CONV, c:c + step]
        for back in range(1, SSM_CONV):
            w = SSM_CONV - 1 - back
            acc = acc + pltpu.roll(ext, back, 0)[SUBLANES:] * cw_ref[w:w + 1, c:c + step]
        ob_ref[:, c:c + step] = _silu(acc).astype(ob_ref.dtype)
    for c in range(conv_dim, BF_WIDTH, step):
        ob_ref[:, c:c + step] = _dot(h, wb_ref[:, c:c + step]).astype(ob_ref.dtype)
    of_ref[...] = _dot(h, wf_ref[...])


def _inproj(x2d, nw, wb, wf, cw, cb, layer, t_len, tm):
    n = x2d.shape[0]
    conv_dim = BF_COLS["sxbc"][1]
    assert BF_COLS["sxbc"][0] == 0 and t_len % tm == 0
    return pl.pallas_call(
        functools.partial(_inproj_kernel, tm, t_len // tm),
        grid=(n // tm,),
        in_specs=[
            pl.BlockSpec((tm, D_MODEL), lambda i: (i, 0)),
            pl.BlockSpec((None, 1, D_MODEL), lambda i: (layer, 0, 0)),
            pl.BlockSpec((None, D_MODEL, BF_WIDTH), lambda i: (layer, 0, 0), pipeline_mode=pl.Buffered(1)),
            pl.BlockSpec((None, D_MODEL, F_WIDTH), lambda i: (layer, 0, 0), pipeline_mode=pl.Buffered(1)),
            pl.BlockSpec((None, SSM_CONV, conv_dim), lambda i: (layer, 0, 0)),
            pl.BlockSpec((None, 1, conv_dim), lambda i: (layer, 0, 0)),
        ],
        out_specs=[
            pl.BlockSpec((tm, BF_WIDTH), lambda i: (i, 0)),
            pl.BlockSpec((tm, F_WIDTH), lambda i: (i, 0)),
        ],
        out_shape=[jax.ShapeDtypeStruct((n, BF_WIDTH), BF16), jax.ShapeDtypeStruct((n, F_WIDTH), F32)],
        scratch_shapes=[pltpu.VMEM((tm + 2 * SUBLANES, conv_dim), F32)],
        compiler_params=_cparams(("arbitrary",)),
        name="inproj",
    )(x2d, nw, wb, wf, cw, cb)


def _la_chunk_head(qh, kh, vh, bh, hsl, kc_ref, bc_ref, st_ref, h):
    C = LA_CHUNK
    row8 = lax.broadcasted_iota(jnp.int32, (SUBLANES, 1), 0)
    colc = lax.broadcasted_iota(jnp.int32, (SUBLANES, C), 1)
    b_last = bh[C - 1:C, :]
    st = st_ref[h]
    o = _dot_nt((qh * jnp.exp(bh)).astype(BF16), st.astype(BF16))
    tiles = []
    for blk in range(C // LA_SUB):
        r0 = blk * LA_SUB
        if blk == 0:
            halves = [jnp.zeros((SUBLANES, C), F32), jnp.zeros((SUBLANES, C), F32)]
        else:
            bs = bh[r0:r0 + 1, :]
            qi = (qh[r0:r0 + LA_SUB] * jnp.exp(bh[r0:r0 + LA_SUB] - bs)).astype(BF16)
            kj = kh[0:r0] * jnp.exp(bs - bh[0:r0])
            kj = jnp.concatenate([kj, jnp.zeros((C - r0, HEAD_W), F32)], axis=0).astype(BF16)
            s = _dot_nt(qi, kj)
            halves = [s[0:SUBLANES], s[SUBLANES:LA_SUB]]
        for j in range(LA_SUB):
            bj = bc_ref[r0 + j:r0 + j + 1, hsl]
            kj = kc_ref[r0 + j:r0 + j + 1, hsl]
            for half in range(2):
                if j >= SUBLANES * (half + 1):
                    continue
                rs = slice(r0 + SUBLANES * half, r0 + SUBLANES * (half + 1))
                d = bh[rs] - bj
                p = qh[rs] * kj
                if j >= SUBLANES * half:
                    m = row8 >= (j - SUBLANES * half)
                    p = jnp.where(m, p * jnp.exp(jnp.where(m, d, 0.0)), 0.0)
                else:
                    p = p * jnp.exp(d)
                r = jnp.sum(p, axis=-1, keepdims=True)
                halves[half] = jnp.where(colc == (r0 + j), r, halves[half])
        tiles += halves
    a = jnp.concatenate(tiles, axis=0).astype(BF16)
    o = o + _dot(a, vh)
    kte = (kh * jnp.exp(b_last - bh)).astype(BF16)
    st_ref[h] = st * jnp.exp(b_last) + _dot_tn(vh, kte)
    return o


def _la_mild_prep(qh, kh, vh, bh, causal):
    C = LA_CHUNK
    b_last = bh[C - 1:C, :]
    b_mid = bh[C // 2 - 1:C // 2, :]
    qm = qh * jnp.exp(bh - b_mid)
    ke = (kh * jnp.exp(b_mid - bh)).astype(BF16)
    a = jnp.where(causal, _dot_nt(qm.astype(BF16), ke), 0.0).astype(BF16)
    kte = (kh * jnp.exp(b_last - bh)).astype(BF16)
    return (qm * jnp.exp(b_mid)).astype(BF16), _dot(a, vh), _dot_tn(vh, kte), jnp.exp(b_last)


def _la_kernel(mode, nchunks, *refs):
    if mode == "gla":
        (q_ref, k_ref, v_ref, gate_ref, gl_ref, up_ref, gb_ref, nw_ref,
         o_ref, st_ref, q_s, k_s, b_s, o_s, kc_ref, bc_ref) = refs
    else:
        (q_ref, v_ref, gate_ref, f_ref, loglb_ref, log1m_ref, onem_ref, nw_ref,
         o_ref, st_ref, q_s, k_s, b_s, o_s, kc_ref, bc_ref) = refs

    @pl.when(pl.program_id(1) == 0)
    def _():
        st_ref[...] = jnp.zeros_like(st_ref)

    C = LA_CHUNK
    causal = _tril_ones(C)
    tril = causal.astype(BF16)

    if mode == "gla":
        q_s[...] = q_ref[...].astype(F32) * (GLA_DK ** -0.5)
        k_s[...] = k_ref[...].astype(F32)
        logit = _dot(gl_ref[...].astype(BF16), up_ref[...]) + gb_ref[...]
        g = _log_sigmoid(logit) * (1.0 / GLA_TEMP)
    else:
        q_s[...] = _silu(q_ref[...].astype(F32)) * (HEAD_W ** -0.5)
        f = f_ref[...]
        e = jnp.exp(-jnp.abs(f))
        u = loglb_ref[...]
        w = log1m_ref[...] + (jnp.minimum(f, 0.0) - jnp.log(1.0 + e))
        g = jnp.maximum(u, w) + _log1p_exp_neg_abs(u - w)
        k_s[...] = onem_ref[...] * (jnp.where(f >= 0.0, e, 1.0) / (1.0 + e))
    spread = jnp.zeros((1, MIX_W), F32)
    for c in range(nchunks):
        b = _cumsum_rows(tril, g[c * C:(c + 1) * C])
        b_s[c * C:(c + 1) * C, :] = b
        b_mid = b[C // 2 - 1:C // 2]
        spread = jnp.maximum(spread, jnp.maximum(b[0:1] - b_mid, b_mid - b[C - 1:C]))
    mild = jnp.max(spread) <= LA_MILD

    @pl.when(mild)
    def _():
        def body(gi, carry):
            base = pl.multiple_of(gi * (LA_GROUP * C), LA_GROUP * C)
            for h in range(HEADS):
                hsl = slice(h * HEAD_W, (h + 1) * HEAD_W)
                rows = [pl.ds(base + j * C, C) for j in range(LA_GROUP)]
                prep = [_la_mild_prep(q_s[rs, hsl], k_s[rs, hsl], v_ref[rs, hsl], b_s[rs, hsl], causal)
                        for rs in rows]
                st = st_ref[h]
                for rs, (qe, o_intra, inc, dec) in zip(rows, prep):
                    o_s[rs, hsl] = o_intra + _dot_nt(qe, st.astype(BF16))
                    st = st * dec + inc
                st_ref[h] = st
            return carry
        lax.fori_loop(0, nchunks // LA_GROUP, body, 0)

    @pl.when(jnp.logical_not(mild))
    def _():
        def body(c, carry):
            rs = pl.ds(pl.multiple_of(c * C, C), C)
            kc_ref[...] = k_s[rs, :]
            bc_ref[...] = b_s[rs, :]
            for h in range(HEADS):
                hsl = slice(h * HEAD_W, (h + 1) * HEAD_W)
                o_s[rs, hsl] = _la_chunk_head(q_s[rs, hsl], k_s[rs, hsl], v_ref[rs, hsl], b_s[rs, hsl],
                                              hsl, kc_ref, bc_ref, st_ref, h)
            return carry
        lax.fori_loop(0, nchunks, body, 0)

    for h in range(HEADS):
        hsl = slice(h * HEAD_W, (h + 1) * HEAD_W)
        o = o_s[:, hsl]
        y = o * lax.rsqrt(jnp.mean(o * o, axis=-1, keepdims=True) + NORM_EPS) * nw_ref[:, hsl]
        o_ref[:, hsl] = (y * _silu(gate_ref[:, hsl].astype(F32))).astype(o_ref.dtype)


def _col_spec(tb, tblocks, width, off):
    assert off % width == 0
    cb = off // width
    return pl.BlockSpec((tb, width), lambda b, t: (b * tblocks + t, cb))


def _row_param(width, layer):
    return pl.BlockSpec((None, 1, width), lambda b, t: (layer, 0, 0))


def _lin_attn(mode, pb, pf, params, layer, bsz, t_len, tb):
    n = bsz * t_len
    tblocks = t_len // tb
    sp = functools.partial(_col_spec, tb, tblocks)
    if mode == "gla":
        up, gb, nw = params
        ins = [pb, pb, pb, pb, pf, up, gb, nw]
        specs = [sp(MIX_W, BF_COLS["gq"][0]), sp(MIX_W, BF_COLS["gk"][0]), sp(MIX_W, BF_COLS["gv"][0]),
                 sp(MIX_W, BF_COLS["gr"][0]), sp(LANES, F_COLS["glow"][0]),
                 pl.BlockSpec((None, LANES, MIX_W), lambda b, t: (layer, 0, 0)),
                 _row_param(MIX_W, layer), _row_param(MIX_W, layer)]
    else:
        loglb, log1m, onem, nw = params
        ins = [pb, pb, pb, pf, loglb, log1m, onem, nw]
        specs = [sp(MIX_W, BF_COLS["hq"][0]), sp(MIX_W, BF_COLS["hi"][0]), sp(MIX_W, BF_COLS["hg"][0]),
                 sp(MIX_W, F_COLS["hf"][0]),
                 _row_param(MIX_W, layer), _row_param(MIX_W, layer), _row_param(MIX_W, layer),
                 _row_param(MIX_W, layer)]
    return pl.pallas_call(
        functools.partial(_la_kernel, mode, tb // LA_CHUNK),
        grid=(bsz, tblocks),
        in_specs=specs,
        out_specs=pl.BlockSpec((tb, MIX_W), lambda b, t: (b * tblocks + t, 0)),
        out_shape=jax.ShapeDtypeStruct((n, MIX_W), BF16),
        scratch_shapes=([pltpu.VMEM((HEADS, HEAD_W, HEAD_W), F32)] + [pltpu.VMEM((tb, MIX_W), F32)] * 4
                        + [pltpu.VMEM((LA_CHUNK, MIX_W), F32)] * 2),
        compiler_params=_cparams(("parallel", "arbitrary")),
        name="lin_attn_" + mode,
    )(*ins)


SSD_B_PARTS = 3
SSD_DT_PARTS = 2
SSD_SEL_COLS = SSM_HEADS * LANES + (SSM_HEADS // 2) * LANES


def _ssd_select():
    k = np.arange(LANES)[:, None]
    n = np.arange(SSD_SEL_COLS)[None, :]
    nb = SSM_HEADS * LANES
    h = n // LANES
    sel_b = (n < nb) & (k < SSD_B_PARTS * SSM_HEADS) & (k % SSM_HEADS == h)
    hh = 2 * ((n - nb) // LANES) + ((n - nb) % LANES >= SSM_P)
    kd = k - SSD_B_PARTS * SSM_HEADS
    sel_d = (n >= nb) & (kd >= 0) & (kd < SSD_DT_PARTS * SSM_HEADS) & (kd % SSM_HEADS == hh)
    return jnp.asarray((sel_b | sel_d).astype(np.float32), dtype=BF16)


def _ssd_kernel(nchunks, xbc_ref, z_ref, dt_ref, sel_ref, dtb_ref, alog_ref, dvec_ref, nw_ref, o_ref, st_ref):
    @pl.when(pl.program_id(1) == 0)
    def _():
        st_ref[...] = jnp.zeros_like(st_ref)

    C = SSD_CHUNK
    causal = _tril_ones(C)
    tril = causal.astype(BF16)
    lane = lax.broadcasted_iota(jnp.int32, (C, LANES), 1)
    row = lax.broadcasted_iota(jnp.int32, (C, LANES), 0)
    lo_lane = lane < SSM_P
    lo_row = row < SSM_P
    head_lane = lane < SSM_HEADS
    a_neg = -jnp.exp(alog_ref[...])
    d_inner = SSM_HEADS * SSM_P
    gn = SSM_GROUPS * SSM_N
    hpg = SSM_HEADS // SSM_GROUPS

    def body(c, carry):
        rs = pl.ds(pl.multiple_of(c * C, C), C)
        dt = _softplus(dt_ref[rs, :] + dtb_ref[...])
        b_col = _cumsum_rows(tril, dt * a_neg)
        b_row = b_col.T
        e_last = jnp.exp(b_col[C - 1:C, :])
        d1 = dt.astype(BF16)
        parts = list(_split3(b_col)) + [d1, (dt - d1.astype(F32)).astype(BF16)]
        packed = jnp.zeros((C, LANES), F32)
        for idx, part in enumerate(parts):
            v = jnp.where(head_lane, part.astype(F32), 0.0)
            packed = packed + (pltpu.roll(v, SSM_HEADS * idx, 1) if idx else v)
        bc = _dot(packed.astype(BF16), sel_ref[...])
        for g in range(SSM_GROUPS):
            bg = xbc_ref[rs, d_inner + g * SSM_N:d_inner + (g + 1) * SSM_N]
            cg = xbc_ref[rs, d_inner + gn + g * SSM_N:d_inner + gn + (g + 1) * SSM_N]
            cbm = jnp.where(causal, _dot_nt(cg, bg), 0.0)
            ys = []
            for pp in range(hpg // 2):
                p = g * (hpg // 2) + pp
                ha, hb = 2 * p, 2 * p + 1
                psl = slice(p * LANES, (p + 1) * LANES)
                x2 = xbc_ref[rs, psl].astype(F32)
                b_a = bc[:, ha * LANES:(ha + 1) * LANES]
                b_b = bc[:, hb * LANES:(hb + 1) * LANES]
                dt_sel = bc[:, (SSM_HEADS + p) * LANES:(SSM_HEADS + p + 1) * LANES]
                xdt = x2 * dt_sel
                xdtb = xdt.astype(BF16)

                def wmat(h, b_i):
                    seg = jnp.minimum(b_i - b_row[h:h + 1, :], 0.0)
                    return (jnp.exp(seg) * cbm).astype(BF16)

                y2 = jnp.where(lo_lane, _dot(wmat(ha, b_a), xdtb), _dot(wmat(hb, b_b), xdtb))
                stp = st_ref[p]
                b_sel = jnp.where(lo_lane, b_a, b_b)
                y2 = y2 + _dot_nt(cg, stp.astype(BF16)) * jnp.exp(b_sel)
                upd = _dot_tn((xdt * jnp.exp(b_sel[C - 1:C, :] - b_sel)).astype(BF16), bg)
                dsel = jnp.where(lo_row, e_last[:, ha:ha + 1], e_last[:, hb:hb + 1])
                st_ref[p] = stp * dsel + upd
                y2 = y2 + dvec_ref[:, psl] * x2
                y2 = y2 * _silu(z_ref[rs, psl].astype(F32))
                ys.append(y2)
            yg = jnp.concatenate(ys, axis=-1)
            var = jnp.mean(yg * yg, axis=-1, keepdims=True)
            gsl = slice(g * 2 * LANES, (g + 1) * 2 * LANES)
            o_ref[rs, gsl] = (yg * lax.rsqrt(var + NORM_EPS) * nw_ref[:, gsl]).astype(o_ref.dtype)
        return carry

    lax.fori_loop(0, nchunks, body, 0)


def _ssd(pb, pf, sel, dtb, alog, dvec, nw, layer, bsz, t_len, tb):
    n = bsz * t_len
    tblocks = t_len // tb
    sp = functools.partial(_col_spec, tb, tblocks)
    conv_dim = SSM_HEADS * SSM_P + 2 * SSM_GROUPS * SSM_N
    d_inner = SSM_HEADS * SSM_P
    return pl.pallas_call(
        functools.partial(_ssd_kernel, tb // SSD_CHUNK),
        grid=(bsz, tblocks),
        in_specs=[sp(conv_dim, BF_COLS["sxbc"][0]), sp(d_inner, BF_COLS["sz"][0]), sp(LANES, F_COLS["sdt"][0]),
                  pl.BlockSpec((LANES, SSD_SEL_COLS), lambda b, t: (0, 0)),
                  _row_param(LANES, layer), _row_param(LANES, layer),
                  _row_param(d_inner, layer), _row_param(d_inner, layer)],
        out_specs=pl.BlockSpec((tb, d_inner), lambda b, t: (b * tblocks + t, 0)),
        out_shape=jax.ShapeDtypeStruct((n, d_inner), BF16),
        scratch_shapes=[pltpu.VMEM((SSM_HEADS // 2, 2 * SSM_P, SSM_N), F32)],
        compiler_params=_cparams(("parallel", "arbitrary")),
        name="ssd",
    )(pb, pb, pf, sel, dtb, alog, dvec, nw)


def _merge_kernel(nsub, yg_ref, yh_ref, ys_ref, gates_ref, x_ref, w1_ref, w2_ref, w3_ref, wo_ref,
                  n2_ref, rwh_ref, rwl_ref, rb_ref, xo_ref, h2_ref, ri_ref, cnt_ref, run_ref):
    @pl.when(pl.program_id(0) == 0)
    def _():
        run_ref[...] = jnp.zeros_like(run_ref)

    sub = x_ref.shape[0] // nsub

    def matmul_stages(rs):
        def branch(k, y_ref, w_ref):
            g = jax.nn.sigmoid(gates_ref[rs, k * D_MODEL:(k + 1) * D_MODEL].astype(F32))
            return g * _dot(y_ref[rs, :], w_ref[...])
        m = branch(0, yg_ref, w1_ref)
        yield
        m = m + branch(1, yh_ref, w2_ref)
        yield
        m = m + branch(2, ys_ref, w3_ref)
        yield
        xo_ref[rs, :] = x_ref[rs, :] + _dot(m.astype(BF16), wo_ref[...])
        yield

    routing = iter(())
    for s in range(nsub):
        rs = slice(s * sub, (s + 1) * sub)
        for _ in matmul_stages(rs):
            next(routing, None)
        for _ in routing:
            pass
        routing = _route(xo_ref, n2_ref, rwh_ref, rwl_ref, rb_ref, h2_ref, ri_ref, run_ref, rs)
    for _ in routing:
        pass
    cnt_ref[...] = run_ref[...]


def _route(xo_ref, n2_ref, rwh_ref, rwl_ref, rb_ref, h2_ref, ri_ref, run_ref, rs):
    xn = xo_ref[rs, :]
    h2 = xn * lax.rsqrt(jnp.mean(xn * xn, axis=-1, keepdims=True) + NORM_EPS) * n2_ref[...]
    h2_ref[rs, :] = h2
    yield
    hi = h2.astype(BF16)
    lo = (h2 - hi.astype(F32)).astype(BF16)
    logits = _dot(hi, rwh_ref[...]) + _dot(lo, rwh_ref[...]) + _dot(hi, rwl_ref[...]) + rb_ref[...]
    yield

    lane = lax.broadcasted_iota(jnp.int32, logits.shape, 1)
    lanef = lane.astype(F32)
    neg = jnp.float32(-1e30)
    big = jnp.float32(1e9)
    gmask = lane < MOE_GROUPS
    gl = jnp.where(gmask, logits, neg)
    gexp = jnp.where(gmask, jnp.exp(gl - jnp.max(gl, axis=-1, keepdims=True)), 0.0)
    gprob = gexp / jnp.sum(gexp, axis=-1, keepdims=True)
    g_p = jnp.max(gprob, axis=-1, keepdims=True)
    g_idx = jnp.min(jnp.where(gmask & (gprob == g_p), lanef, big), axis=-1, keepdims=True)
    yield
    lo_l = MOE_GROUPS + EXPERTS_PER_GROUP * g_idx
    emask = (lanef >= lo_l) & (lanef < lo_l + EXPERTS_PER_GROUP)
    el = jnp.where(emask, logits, neg)
    eexp = jnp.where(emask, jnp.exp(el - jnp.max(el, axis=-1, keepdims=True)), 0.0)
    eprob = eexp / jnp.sum(eexp, axis=-1, keepdims=True)
    p1 = jnp.max(jnp.where(emask, eprob, -1.0), axis=-1, keepdims=True)
    i1 = jnp.min(jnp.where(emask & (eprob == p1), lanef, big), axis=-1, keepdims=True)
    rest = emask & (lanef != i1)
    p2 = jnp.max(jnp.where(rest, eprob, -1.0), axis=-1, keepdims=True)
    i2 = jnp.min(jnp.where(rest & (eprob == p2), lanef, big), axis=-1, keepdims=True)
    den = p1 + p2
    w_a = g_p * (p1 / den)
    w_b = g_p * (p2 / den)
    yield

    nrow = logits.shape[0]
    r_i = lax.broadcasted_iota(jnp.int32, (nrow, nrow), 0)
    c_i = lax.broadcasted_iota(jnp.int32, (nrow, nrow), 1)
    before = (r_i > c_i).astype(BF16)
    oh1 = (lanef == i1).astype(F32)
    oh2 = (lanef == i2).astype(F32)
    run = run_ref[...]
    c1 = jnp.sum(oh1, axis=0, keepdims=True)
    pos1 = jnp.sum(oh1 * (run + _dot(before, oh1.astype(BF16))), axis=-1, keepdims=True)
    pos2 = jnp.sum(oh2 * (run + c1 + _dot(before, oh2.astype(BF16))), axis=-1, keepdims=True)
    run_ref[...] = run + c1 + jnp.sum(oh2, axis=0, keepdims=True)

    out = jnp.zeros_like(logits)
    for k, val in enumerate((i1 - MOE_GROUPS, i2 - MOE_GROUPS, w_a, w_b, pos1, pos2)):
        out = jnp.where(lane == k, val, out)
    ri_ref[rs, :] = out


def _merge(yg, yh, ys, pb, x2d, w1, w2, w3, wo, n2, rwh, rwl, rb, layer, tm, nsub=MERGE_SUBTILES):
    n = x2d.shape[0]
    gates_blk = BF_COLS["gates"][0] // BF_COLS["gates"][1]

    def wspec(k):
        return pl.BlockSpec((None, k, D_MODEL), lambda i: (layer, 0, 0))

    return pl.pallas_call(
        functools.partial(_merge_kernel, nsub),
        grid=(n // tm,),
        in_specs=[pl.BlockSpec((tm, MIX_W), lambda i: (i, 0)),
                  pl.BlockSpec((tm, MIX_W), lambda i: (i, 0)),
                  pl.BlockSpec((tm, D_MODEL), lambda i: (i, 0)),
                  pl.BlockSpec((tm, 3 * D_MODEL), lambda i: (i, gates_blk)),
                  pl.BlockSpec((tm, D_MODEL), lambda i: (i, 0)),
                  wspec(MIX_W), wspec(MIX_W), wspec(D_MODEL), wspec(D_MODEL),
                  pl.BlockSpec((None, 1, D_MODEL), lambda i: (layer, 0, 0)),
                  pl.BlockSpec((None, D_MODEL, LANES), lambda i: (layer, 0, 0)),
                  pl.BlockSpec((None, D_MODEL, LANES), lambda i: (layer, 0, 0)),
                  pl.BlockSpec((None, 1, LANES), lambda i: (layer, 0, 0))],
        out_specs=[pl.BlockSpec((tm, D_MODEL), lambda i: (i, 0)),
                   pl.BlockSpec((tm, D_MODEL), lambda i: (i, 0)),
                   pl.BlockSpec((tm, LANES), lambda i: (i, 0)),
                   pl.BlockSpec((1, LANES), lambda i: (0, 0))],
        out_shape=[jax.ShapeDtypeStruct((n, D_MODEL), F32),
                   jax.ShapeDtypeStruct((n, D_MODEL), F32),
                   jax.ShapeDtypeStruct((n, LANES), F32),
                   jax.ShapeDtypeStruct((1, LANES), F32)],
        scratch_shapes=[pltpu.VMEM((1, LANES), F32)],
        compiler_params=_cparams(("arbitrary",)),
        name="merge_router",
    )(yg, yh, ys, pb, x2d, w1, w2, w3, wo, n2, rwh, rwl, rb)


def _dispatch_kernel(tp, nsteps, blk, idx_hbm, pstart_ref, pend_ref, h_ref, xb_out, idx_smem, zbuf, isem, rsem,
                     zsem):
    i = pl.program_id(0)
    slot = i % 2

    def idx_copy(step, s):
        return pltpu.make_async_copy(idx_hbm.at[step], idx_smem.at[s], isem.at[s])

    def row_copy(r, dst):
        return pltpu.make_async_copy(h_ref.at[pl.ds(r, 1)], xb_out.at[pl.ds(dst, 1)], rsem)

    def zero_copy(start):
        if not isinstance(start, int):
            start = pl.multiple_of(start, blk)
        return pltpu.make_async_copy(zbuf, xb_out.at[pl.ds(start, blk)], zsem)

    @pl.when(i == 0)
    def _():
        idx_copy(0, 0).start()
        if nsteps > 1:
            idx_copy(1, 1).start()
        zbuf[...] = jnp.zeros_like(zbuf)
        total = xb_out.shape[0]
        todo = [(pend_ref[e] > pstart_ref[e], pend_ref[e] - blk) for e in range(N_EXPERTS)]
        todo += [(start >= pend_ref[N_EXPERTS - 1], start) for start in range(total - N_EXPERTS * blk, total, blk)]
        for cond, start in todo:
            @pl.when(cond)
            def _():
                zero_copy(start).start()
        for cond, start in todo:
            @pl.when(cond)
            def _():
                zero_copy(start).wait()

    idx_copy(i, slot).wait()

    for r in range(tp):
        for k in range(2):
            row_copy(r, idx_smem[slot, k * tp + r]).start()

    @pl.when(i + 2 < nsteps)
    def _():
        idx_copy(i + 2, slot).start()

    def drain(r, carry):
        for k in range(2):
            row_copy(0, 0).wait()
        return carry

    lax.fori_loop(0, tp, drain, 0, unroll=8)


def _dispatch(dest_tiles, pstart, pend, h2, rows, tp, blk):
    n = h2.shape[0]
    return pl.pallas_call(
        functools.partial(_dispatch_kernel, tp, n // tp, blk),
        grid=(n // tp,),
        in_specs=[pl.BlockSpec(memory_space=pl.ANY),
                  pl.BlockSpec(memory_space=pltpu.SMEM),
                  pl.BlockSpec(memory_space=pltpu.SMEM),
                  pl.BlockSpec((tp, D_MODEL), lambda i: (i, 0))],
        out_specs=pl.BlockSpec(memory_space=pl.ANY),
        out_shape=jax.ShapeDtypeStruct((rows, D_MODEL), F32),
        scratch_shapes=[pltpu.SMEM((2, 2 * tp), jnp.int32),
                        pltpu.VMEM((blk, D_MODEL), F32),
                        pltpu.SemaphoreType.DMA((2,)),
                        pltpu.SemaphoreType.DMA(()),
                        pltpu.SemaphoreType.DMA(())],
        compiler_params=_cparams(("arbitrary",)),
        name="moe_dispatch",
    )(dest_tiles, pstart, pend, h2)


def _ffn_kernel(be_ref, nb_ref, x_ref, wg_ref, wu_ref, wd_ref, o_ref, wg_s, wu_s, wd_s):
    i = pl.program_id(0)
    used = i < nb_ref[0]

    @pl.when(used & ((i == 0) | (be_ref[i] != be_ref[jnp.maximum(i - 1, 0)])))
    def _():
        wg_s[...] = wg_ref[...].astype(BF16)
        wu_s[...] = wu_ref[...].astype(BF16)
        wd_s[...] = wd_ref[...].astype(BF16)

    @pl.when(used)
    def _():
        x = x_ref[...].astype(BF16)
        hid = _silu(_dot(x, wg_s[...])) * _dot(x, wu_s[...])
        o_ref[...] = _dot(hid.astype(BF16), wd_s[...])

    @pl.when(jnp.logical_not(used))
    def _():
        o_ref[...] = jnp.zeros_like(o_ref)


def _ffn(block_e, nblk_used, xb, wg, wu, wd, layer, blk):
    rows = xb.shape[0]

    def x_map(i, be, nb):
        return (jnp.minimum(i, nb[0] - 1), 0)

    def w_map(i, be, nb):
        return (layer, be[jnp.minimum(i, nb[0] - 1)], 0, 0)

    grid_spec = pltpu.PrefetchScalarGridSpec(
        num_scalar_prefetch=2,
        grid=(rows // blk,),
        in_specs=[pl.BlockSpec((blk, D_MODEL), x_map),
                  pl.BlockSpec((None, None, D_MODEL, MOE_FF), w_map),
                  pl.BlockSpec((None, None, D_MODEL, MOE_FF), w_map),
                  pl.BlockSpec((None, None, MOE_FF, D_MODEL), w_map)],
        out_specs=pl.BlockSpec((blk, D_MODEL), lambda i, be, nb: (i, 0)),
        scratch_shapes=[pltpu.VMEM((D_MODEL, MOE_FF), BF16),
                        pltpu.VMEM((D_MODEL, MOE_FF), BF16),
                        pltpu.VMEM((MOE_FF, D_MODEL), BF16)],
    )
    return pl.pallas_call(
        _ffn_kernel,
        grid_spec=grid_spec,
        out_shape=jax.ShapeDtypeStruct((rows, D_MODEL), F32),
        compiler_params=_cparams(("arbitrary",)),
        name="moe_ffn",
    )(block_e, nblk_used, xb, wg, wu, wd)


def _combine_kernel(tc, nsteps, final, idx_hbm, yb_hbm, x_ref, w_ref, fw_ref, o_ref, idx_smem, ybuf, isem, rsem):
    i = pl.program_id(0)
    slot = i % 2

    def idx_copy(step, s):
        return pltpu.make_async_copy(idx_hbm.at[step], idx_smem.at[s], isem.at[s])

    def row_copy(src, r, s):
        return pltpu.make_async_copy(yb_hbm.at[pl.ds(src, 1)], ybuf.at[s, pl.ds(r, 1)], rsem.at[s])

    def issue_rows(s):
        for r in range(2 * tc):
            row_copy(idx_smem[s, r], r, s).start()

    @pl.when(i == 0)
    def _():
        idx_copy(0, 0).start()
        idx_copy(0, 0).wait()
        issue_rows(0)
        if nsteps > 1:
            idx_copy(1, 1).start()

    @pl.when(i + 1 < nsteps)
    def _():
        idx_copy(i + 1, 1 - slot).wait()
        issue_rows(1 - slot)

    @pl.when(i + 2 < nsteps)
    def _():
        idx_copy(i + 2, slot).start()

    def drain(r, carry):
        row_copy(0, 0, slot).wait()
        return carry

    lax.fori_loop(0, 2 * tc, drain, 0, unroll=8)
    w = w_ref[...]
    out = x_ref[...] + w[:, 2:3] * ybuf[slot, 0:tc, :] + w[:, 3:4] * ybuf[slot, tc:2 * tc, :]
    if final:
        out = out * lax.rsqrt(jnp.mean(out * out, axis=-1, keepdims=True) + NORM_EPS) * fw_ref[...]
    o_ref[...] = out


def _combine(dest_tiles, yb, x2d, ri, fw, final, tc):
    n = x2d.shape[0]
    return pl.pallas_call(
        functools.partial(_combine_kernel, tc, n // tc, final),
        grid=(n // tc,),
        in_specs=[pl.BlockSpec(memory_space=pl.ANY),
                  pl.BlockSpec(memory_space=pl.ANY),
                  pl.BlockSpec((tc, D_MODEL), lambda i: (i, 0)),
                  pl.BlockSpec((tc, LANES), lambda i: (i, 0)),
                  pl.BlockSpec((1, D_MODEL), lambda i: (0, 0))],
        out_specs=pl.BlockSpec((tc, D_MODEL), lambda i: (i, 0)),
        out_shape=jax.ShapeDtypeStruct((n, D_MODEL), F32),
        scratch_shapes=[pltpu.SMEM((2, 2 * tc), jnp.int32),
                        pltpu.VMEM((2, 2 * tc, D_MODEL), F32),
                        pltpu.SemaphoreType.DMA((2,)),
                        pltpu.SemaphoreType.DMA((2,))],
        compiler_params=_cparams(("arbitrary",)),
        name="moe_combine",
    )(dest_tiles, yb, x2d, ri, fw)


def _route_slots(ri, cnt, blk):
    n = ri.shape[0]
    e = ri[:, 0:2].astype(jnp.int32)
    pos = ri[:, 4:6].astype(jnp.int32)
    counts = cnt[0, MOE_GROUPS:MOE_GROUPS + N_EXPERTS].astype(jnp.int32)
    padded = (counts + blk - 1) // blk * blk
    pend = jnp.cumsum(padded)
    pstart = pend - padded
    experts = jnp.arange(N_EXPERTS, dtype=jnp.int32)
    dest = (jnp.sum(jnp.where(e[:, :, None] == experts, pstart, 0), axis=-1) + pos).astype(jnp.int32)
    rows = 2 * n + N_EXPERTS * blk
    nblk = rows // blk
    block_start = jnp.arange(nblk, dtype=jnp.int32) * blk
    block_e = jnp.minimum(jnp.sum((block_start[:, None] >= pend[None, :]).astype(jnp.int32), axis=1),
                          N_EXPERTS - 1).astype(jnp.int32)
    nblk_used = (pend[-1] // blk).astype(jnp.int32).reshape(1)
    return dest, block_e, nblk_used, pstart.astype(jnp.int32), pend.astype(jnp.int32), rows


def _tile_slots(dest, tile):
    n = dest.shape[0]
    return dest.reshape(n // tile, tile, 2).transpose(0, 2, 1).reshape(n // tile, 2 * tile)


def _pad_heads(w, heads, width):
    lead = w.shape[:-1]
    w = w.reshape(lead + (heads, width))
    w = jnp.pad(w, [(0, 0)] * len(lead) + [(0, 0), (0, HEAD_W - width)])
    return w.reshape(lead + (heads * HEAD_W,))


def _pad_last(w, width):
    return jnp.pad(w, [(0, 0)] * (w.ndim - 1) + [(0, width - w.shape[-1])])


def _prep_inproj(w_in):
    gla_kw, gla_vw, hw = HEADS * GLA_DK, MIX_W, MIX_W
    d_inner = SSM_HEADS * SSM_P
    conv_dim = d_inner + 2 * SSM_GROUPS * SSM_N
    sizes = (gla_kw, gla_kw, gla_vw, GLA_RANK, gla_vw, hw, hw, hw, hw, d_inner, conv_dim, SSM_HEADS, 3 * D_MODEL)
    offs = [0]
    for s in sizes:
        offs.append(offs[-1] + s)
    seg = {name: w_in[..., offs[i]:offs[i + 1]] for i, name in enumerate(
        ("gq", "gk", "gv", "glow", "gr", "hq", "hf", "hi", "hg", "sz", "sxbc", "sdt", "gates"))}
    seg["gq"] = _pad_heads(seg["gq"], HEADS, GLA_DK)
    seg["gk"] = _pad_heads(seg["gk"], HEADS, GLA_DK)
    seg["glow"] = _pad_last(seg["glow"], LANES)
    seg["sdt"] = _pad_last(seg["sdt"], LANES)
    wb = jnp.concatenate([seg[k] for k in sorted(BF_COLS, key=lambda k: BF_COLS[k][0])], axis=-1).astype(BF16)
    wf = jnp.concatenate([seg[k] for k in sorted(F_COLS, key=lambda k: F_COLS[k][0])], axis=-1).astype(BF16)
    return wb, wf


def kernel(x, norm1_w, w_in, gla_gk_up, gla_gk_bias, gla_norm_w, hgrn_lb_logits, hgrn_norm_w,
           ssm_conv_w, ssm_conv_b, ssm_dt_bias, ssm_a_log, ssm_d, ssm_norm_w,
           w_br_gla, w_br_hgrn, w_br_ssm, w_out, norm2_w,
           router_group_w, router_group_b, router_expert_w, router_expert_b,
           moe_w_gate, moe_w_up, moe_w_down, final_norm_w):
    bsz, t_len, d = x.shape
    n = bsz * t_len
    depth = w_in.shape[0]
    x2d = x.reshape(n, d).astype(F32)

    wb, wf = _prep_inproj(w_in)
    row = lambda a: a.astype(F32)[:, None, :]
    norm1 = row(norm1_w)
    norm2 = row(norm2_w)
    up = _pad_heads(jnp.pad(gla_gk_up, ((0, 0), (0, LANES - GLA_RANK), (0, 0))), HEADS, GLA_DK).astype(BF16)
    gb = row(_pad_heads(gla_gk_bias, HEADS, GLA_DK))
    gnw = row(gla_norm_w)
    lower = jnp.cumsum(jax.nn.softmax(hgrn_lb_logits.astype(F32), axis=0), axis=0)
    lb = lower - lower[0:1]
    loglb, log1m, onem = row(jnp.log(lb)), row(jnp.log1p(-lb)), row(1.0 - lb)
    hnw = row(hgrn_norm_w)
    cw = ssm_conv_w.astype(F32)
    cb = row(ssm_conv_b)
    dtb = row(_pad_last(ssm_dt_bias, LANES))
    alog = row(_pad_last(ssm_a_log, LANES))
    dvec = row(jnp.repeat(ssm_d, SSM_P, axis=-1))
    snw = row(ssm_norm_w)
    sel = _ssd_select()
    w1, w2, w3, wo = (w.astype(BF16) for w in (w_br_gla, w_br_hgrn, w_br_ssm, w_out))
    rw = _pad_last(jnp.concatenate([router_group_w, router_expert_w], axis=-1).astype(F32), LANES)
    rwh = rw.astype(BF16)
    rwl = (rw - rwh.astype(F32)).astype(BF16)
    rb = row(_pad_last(jnp.concatenate([router_group_b, router_expert_b], axis=-1), LANES))
    fw = final_norm_w.astype(F32)[None, :]

    for l in range(depth):
        pb, pf = _inproj(x2d, norm1, wb, wf, cw, cb, l, t_len, INPROJ_TILE)
        yg = _lin_attn("gla", pb, pf, (up, gb, gnw), l, bsz, t_len, LA_TILE)
        yh = _lin_attn("hgrn", pb, pf, (loglb, log1m, onem, hnw), l, bsz, t_len, LA_TILE)
        ys = _ssd(pb, pf, sel, dtb, alog, dvec, snw, l, bsz, t_len, SSD_TILE)
        x2d, h2, ri, cnt = _merge(yg, yh, ys, pb, x2d, w1, w2, w3, wo, norm2, rwh, rwl, rb, l, MERGE_TILE)
        dest, block_e, nblk_used, pstart, pend, rows = _route_slots(ri, cnt, MOE_BLOCK)
        xb = _dispatch(_tile_slots(dest, DISPATCH_TILE), pstart, pend, h2, rows, DISPATCH_TILE, MOE_BLOCK)
        yb = _ffn(block_e, nblk_used, xb, moe_w_gate, moe_w_up, moe_w_down, l, MOE_BLOCK)
        x2d = _combine(_tile_slots(dest, COMBINE_TILE), yb, x2d, ri, fw, l == depth - 1, COMBINE_TILE)
    return x2d.reshape(bsz, t_len, d)
```

```python
import functools

import jax
import jax.numpy as jnp
import numpy as np
from jax import lax
from jax.experimental import pallas as pl
from jax.experimental.pallas import tpu as pltpu

F32 = jnp.float32
BF16 = jnp.bfloat16

D_MODEL = 1024
DEPTH = 4
NORM_EPS = 1e-6
HEADS = 4
HEAD_W = 128
MIX_W = HEADS * HEAD_W
GLA_DK = 64
GLA_RANK = 16
GLA_TEMP = 16.0
SSM_HEADS = 16
SSM_P = 64
SSM_GROUPS = 4
SSM_N = 128
SSM_CONV = 4
N_EXPERTS = 32
EXPERTS_PER_GROUP = 8
MOE_GROUPS = 4
MOE_FF = 512

LANES = 128
SUBLANES = 8

LA_CHUNK = 64
LA_SUB = 16
LA_GROUP = 8
LA_MILD = 60.0
SSD_CHUNK = 128

BF_COLS = dict(sxbc=(0, 2048), sz=(2048, 1024), gates=(3072, 3072), gv=(6144, 512), gr=(6656, 512),
               hq=(7168, 512), hi=(7680, 512), hg=(8192, 512), gq=(8704, 512), gk=(9216, 512))
BF_WIDTH = 9728
F_COLS = dict(hf=(0, 512), glow=(512, 128), sdt=(640, 128))
F_WIDTH = 768

VMEM_LIMIT = 56 * 1024 * 1024

INPROJ_TILE = 256
LA_TILE = 512
SSD_TILE = 512
MERGE_TILE = 1024
MERGE_SUBTILES = 4
MOE_BLOCK = 512
DISPATCH_TILE = 512
COMBINE_TILE = 256


def _cparams(sem):
    return pltpu.CompilerParams(dimension_semantics=sem, vmem_limit_bytes=VMEM_LIMIT)


def _log1p_exp_neg_abs(x):
    return jnp.log(1.0 + jnp.exp(-jnp.abs(x)))


def _log_sigmoid(x):
    return jnp.minimum(x, 0.0) - _log1p_exp_neg_abs(x)


def _softplus(x):
    return jnp.maximum(x, 0.0) + _log1p_exp_neg_abs(x)


def _silu(x):
    return x * jax.nn.sigmoid(x)


def _dot(a, b):
    return jnp.dot(a, b, preferred_element_type=F32)


def _dot_nt(a, b):
    return lax.dot_general(a, b, (((1,), (1,)), ((), ())), preferred_element_type=F32)


def _dot_tn(a, b):
    return lax.dot_general(a, b, (((0,), (0,)), ((), ())), preferred_element_type=F32)


def _split3(x):
    x1 = x.astype(BF16)
    r1 = x - x1.astype(F32)
    x2 = r1.astype(BF16)
    r2 = r1 - x2.astype(F32)
    return x1, x2, r2.astype(BF16)


def _cumsum_rows(tril, x):
    x1, x2, x3 = _split3(x)
    return _dot(tril, x1) + _dot(tril, x2) + _dot(tril, x3)


def _tril_ones(n):
    r = lax.broadcasted_iota(jnp.int32, (n, n), 0)
    c = lax.broadcasted_iota(jnp.int32, (n, n), 1)
    return r >= c


def _inproj_kernel(tm, tiles_per_seq, x_ref, nw_ref, wb_ref, wf_ref, cw_ref, cb_ref, ob_ref, of_ref, ubuf):
    i = pl.program_id(0)
    x = x_ref[...]
    h = x * lax.rsqrt(jnp.mean(x * x, axis=-1, keepdims=True) + NORM_EPS) * nw_ref[...]
    h = h.astype(BF16)
    step = 512
    conv_dim = BF_COLS["sxbc"][1]

    @pl.when(i % tiles_per_seq == 0)
    def _():
        ubuf[0:SUBLANES, :] = jnp.zeros((SUBLANES, conv_dim), F32)

    @pl.when(i % tiles_per_seq != 0)
    def _():
        ubuf[0:SUBLANES, :] = ubuf[tm:tm + SUBLANES, :]

    for c in range(0, conv_dim, step):
        ubuf[SUBLANES:SUBLANES + tm, c:c + step] = _dot(h, wb_ref[:, c:c + step])
    for c in range(0, conv_dim, step):
        ext = ubuf[0:SUBLANES + tm, c:c + step]
        acc = cb_ref[:, c:c + step] + ext[SUBLANES:] * cw_ref[SSM_CONV - 1:SSM_CONV, c:c + step]
        for back in range(1, SSM_CONV):
            w = SSM_CONV - 1 - back
            acc = acc + pltpu.roll(ext, back, 0)[SUBLANES:] * cw_ref[w:w + 1, c:c + step]
        ob_ref[:, c:c + step] = _silu(acc).astype(ob_ref.dtype)
    for c in range(conv_dim, BF_WIDTH, step):
        ob_ref[:, c:c + step] = _dot(h, wb_ref[:, c:c + step]).astype(ob_ref.dtype)
    of_ref[...] = _dot(h, wf_ref[...])


def _inproj(x2d, nw, wb, wf, cw, cb, layer, t_len, tm):
    n = x2d.shape[0]
    conv_dim = BF_COLS["sxbc"][1]
    assert BF_COLS["sxbc"][0] == 0 and t_len % tm == 0
    return pl.pallas_call(
        functools.partial(_inproj_kernel, tm, t_len // tm),
        grid=(n // tm,),
        in_specs=[
            pl.BlockSpec((tm, D_MODEL), lambda i: (i, 0)),
            pl.BlockSpec((None, 1, D_MODEL), lambda i: (layer, 0, 0)),
            pl.BlockSpec((None, D_MODEL, BF_WIDTH), lambda i: (layer, 0, 0), pipeline_mode=pl.Buffered(1)),
            pl.BlockSpec((None, D_MODEL, F_WIDTH), lambda i: (layer, 0, 0), pipeline_mode=pl.Buffered(1)),
            pl.BlockSpec((None, SSM_CONV, conv_dim), lambda i: (layer, 0, 0)),
            pl.BlockSpec((None, 1, conv_dim), lambda i: (layer, 0, 0)),
        ],
        out_specs=[
            pl.BlockSpec((tm, BF_WIDTH), lambda i: (i, 0)),
            pl.BlockSpec((tm, F_WIDTH), lambda i: (i, 0)),
        ],
        out_shape=[jax.ShapeDtypeStruct((n, BF_WIDTH), BF16), jax.ShapeDtypeStruct((n, F_WIDTH), F32)],
        scratch_shapes=[pltpu.VMEM((tm + 2 * SUBLANES, conv_dim), F32)],
        compiler_params=_cparams(("arbitrary",)),
        name="inproj",
    )(x2d, nw, wb, wf, cw, cb)


def _la_chunk_head(qh, kh, vh, bh, hsl, kc_ref, bc_ref, st_ref, h):
    C = LA_CHUNK
    row8 = lax.broadcasted_iota(jnp.int32, (SUBLANES, 1), 0)
    colc = lax.broadcasted_iota(jnp.int32, (SUBLANES, C), 1)
    b_last = bh[C - 1:C, :]
    st = st_ref[h]
    o = _dot_nt((qh * jnp.exp(bh)).astype(BF16), st.astype(BF16))
    tiles = []
    for blk in range(C // LA_SUB):
        r0 = blk * LA_SUB
        if blk == 0:
            halves = [jnp.zeros((SUBLANES, C), F32), jnp.zeros((SUBLANES, C), F32)]
        else:
            bs = bh[r0:r0 + 1, :]
            qi = (qh[r0:r0 + LA_SUB] * jnp.exp(bh[r0:r0 + LA_SUB] - bs)).astype(BF16)
            kj = kh[0:r0] * jnp.exp(bs - bh[0:r0])
            kj = jnp.concatenate([kj, jnp.zeros((C - r0, HEAD_W), F32)], axis=0).astype(BF16)
            s = _dot_nt(qi, kj)
            halves = [s[0:SUBLANES], s[SUBLANES:LA_SUB]]
        for j in range(LA_SUB):
            bj = bc_ref[r0 + j:r0 + j + 1, hsl]
            kj = kc_ref[r0 + j:r0 + j + 1, hsl]
            for half in range(2):
                if j >= SUBLANES * (half + 1):
                    continue
                rs = slice(r0 + SUBLANES * half, r0 + SUBLANES * (half + 1))
                d = bh[rs] - bj
                p = qh[rs] * kj
                if j >= SUBLANES * half:
                    m = row8 >= (j - SUBLANES * half)
                    p = jnp.where(m, p * jnp.exp(jnp.where(m, d, 0.0)), 0.0)
                else:
                    p = p * jnp.exp(d)
                r = jnp.sum(p, axis=-1, keepdims=True)
                halves[half] = jnp.where(colc == (r0 + j), r, halves[half])
        tiles += halves
    a = jnp.concatenate(tiles, axis=0).astype(BF16)
    o = o + _dot(a, vh)
    kte = (kh * jnp.exp(b_last - bh)).astype(BF16)
    st_ref[h] = st * jnp.exp(b_last) + _dot_tn(vh, kte)
    return o


def _la_mild_prep(qh, kh, vh, bh, causal):
    C = LA_CHUNK
    b_last = bh[C - 1:C, :]
    b_mid = bh[C // 2 - 1:C // 2, :]
    qm = qh * jnp.exp(bh - b_mid)
    ke = (kh * jnp.exp(b_mid - bh)).astype(BF16)
    a = jnp.where(causal, _dot_nt(qm.astype(BF16), ke), 0.0).astype(BF16)
    kte = (kh * jnp.exp(b_last - bh)).astype(BF16)
    return (qm * jnp.exp(b_mid)).astype(BF16), _dot(a, vh), _dot_tn(vh, kte), jnp.exp(b_last)


def _la_kernel(mode, nchunks, *refs):
    if mode == "gla":
        (q_ref, k_ref, v_ref, gate_ref, gl_ref, up_ref, gb_ref, nw_ref,
         o_ref, st_ref, q_s, k_s, b_s, o_s, kc_ref, bc_ref) = refs
    else:
        (q_ref, v_ref, gate_ref, f_ref, loglb_ref, log1m_ref, onem_ref, nw_ref,
         o_ref, st_ref, q_s, k_s, b_s, o_s, kc_ref, bc_ref) = refs

    @pl.when(pl.program_id(1) == 0)
    def _():
        st_ref[...] = jnp.zeros_like(st_ref)

    C = LA_CHUNK
    causal = _tril_ones(C)
    tril = causal.astype(BF16)

    if mode == "gla":
        q_s[...] = q_ref[...].astype(F32) * (GLA_DK ** -0.5)
        k_s[...] = k_ref[...].astype(F32)
        logit = _dot(gl_ref[...].astype(BF16), up_ref[...]) + gb_ref[...]
        g = _log_sigmoid(logit) * (1.0 / GLA_TEMP)
    else:
        q_s[...] = _silu(q_ref[...].astype(F32)) * (HEAD_W ** -0.5)
        f = f_ref[...]
        e = jnp.exp(-jnp.abs(f))
        u = loglb_ref[...]
        w = log1m_ref[...] + (jnp.minimum(f, 0.0) - jnp.log(1.0 + e))
        g = jnp.maximum(u, w) + _log1p_exp_neg_abs(u - w)
        k_s[...] = onem_ref[...] * (jnp.where(f >= 0.0, e, 1.0) / (1.0 + e))
    spread = jnp.zeros((1, MIX_W), F32)
    for c in range(nchunks):
        b = _cumsum_rows(tril, g[c * C:(c + 1) * C])
        b_s[c * C:(c + 1) * C, :] = b
        b_mid = b[C // 2 - 1:C // 2]
        spread = jnp.maximum(spread, jnp.maximum(b[0:1] - b_mid, b_mid - b[C - 1:C]))
    mild = jnp.max(spread) <= LA_MILD

    @pl.when(mild)
    def _():
        def body(gi, carry):
            base = pl.multiple_of(gi * (LA_GROUP * C), LA_GROUP * C)
            for h in range(HEADS):
                hsl = slice(h * HEAD_W, (h + 1) * HEAD_W)
                rows = [pl.ds(base + j * C, C) for j in range(LA_GROUP)]
                prep = [_la_mild_prep(q_s[rs, hsl], k_s[rs, hsl], v_ref[rs, hsl], b_s[rs, hsl], causal)
                        for rs in rows]
                st = st_ref[h]
                for rs, (qe, o_intra, inc, dec) in zip(rows, prep):
                    o_s[rs, hsl] = o_intra + _dot_nt(qe, st.astype(BF16))
                    st = st * dec + inc
                st_ref[h] = st
            return carry
        lax.fori_loop(0, nchunks // LA_GROUP, body, 0)

    @pl.when(jnp.logical_not(mild))
    def _():
        def body(c, carry):
            rs = pl.ds(pl.multiple_of(c * C, C), C)
            kc_ref[...] = k_s[rs, :]
            bc_ref[...] = b_s[rs, :]
            for h in range(HEADS):
                hsl = slice(h * HEAD_W, (h + 1) * HEAD_W)
                o_s[rs, hsl] = _la_chunk_head(q_s[rs, hsl], k_s[rs, hsl], v_ref[rs, hsl], b_s[rs, hsl],
                                              hsl, kc_ref, bc_ref, st_ref, h)
            return carry
        lax.fori_loop(0, nchunks, body, 0)

    for h in range(HEADS):
        hsl = slice(h * HEAD_W, (h + 1) * HEAD_W)
        o = o_s[:, hsl]
        y = o * lax.rsqrt(jnp.mean(o * o, axis=-1, keepdims=True) + NORM_EPS) * nw_ref[:, hsl]
        o_ref[:, hsl] = (y * _silu(gate_ref[:, hsl].astype(F32))).astype(o_ref.dtype)


def _col_spec(tb, tblocks, width, off):
    assert off % width == 0
    cb = off // width
    return pl.BlockSpec((tb, width), lambda b, t: (b * tblocks + t, cb))


def _row_param(width, layer):
    return pl.BlockSpec((None, 1, width), lambda b, t: (layer, 0, 0))


def _lin_attn(mode, pb, pf, params, layer, bsz, t_len, tb):
    n = bsz * t_len
    tblocks = t_len // tb
    sp = functools.partial(_col_spec, tb, tblocks)
    if mode == "gla":
        up, gb, nw = params
        ins = [pb, pb, pb, pb, pf, up, gb, nw]
        specs = [sp(MIX_W, BF_COLS["gq"][0]), sp(MIX_W, BF_COLS["gk"][0]), sp(MIX_W, BF_COLS["gv"][0]),
                 sp(MIX_W, BF_COLS["gr"][0]), sp(LANES, F_COLS["glow"][0]),
                 pl.BlockSpec((None, LANES, MIX_W), lambda b, t: (layer, 0, 0)),
                 _row_param(MIX_W, layer), _row_param(MIX_W, layer)]
    else:
        loglb, log1m, onem, nw = params
        ins = [pb, pb, pb, pf, loglb, log1m, onem, nw]
        specs = [sp(MIX_W, BF_COLS["hq"][0]), sp(MIX_W, BF_COLS["hi"][0]), sp(MIX_W, BF_COLS["hg"][0]),
                 sp(MIX_W, F_COLS["hf"][0]),
                 _row_param(MIX_W, layer), _row_param(MIX_W, layer), _row_param(MIX_W, layer),
                 _row_param(MIX_W, layer)]
    return pl.pallas_call(
        functools.partial(_la_kernel, mode, tb // LA_CHUNK),
        grid=(bsz, tblocks),
        in_specs=specs,
        out_specs=pl.BlockSpec((tb, MIX_W), lambda b, t: (b * tblocks + t, 0)),
        out_shape=jax.ShapeDtypeStruct((n, MIX_W), BF16),
        scratch_shapes=([pltpu.VMEM((HEADS, HEAD_W, HEAD_W), F32)] + [pltpu.VMEM((tb, MIX_W), F32)] * 4
                        + [pltpu.VMEM((LA_CHUNK, MIX_W), F32)] * 2),
        compiler_params=_cparams(("parallel", "arbitrary")),
        name="lin_attn_" + mode,
    )(*ins)


SSD_B_PARTS = 3
SSD_DT_PARTS = 2
SSD_SEL_COLS = SSM_HEADS * LANES + (SSM_HEADS // 2) * LANES


def _ssd_select():
    k = np.arange(LANES)[:, None]
    n = np.arange(SSD_SEL_COLS)[None, :]
    nb = SSM_HEADS * LANES
    h = n // LANES
    sel_b = (n < nb) & (k < SSD_B_PARTS * SSM_HEADS) & (k % SSM_HEADS == h)
    hh = 2 * ((n - nb) // LANES) + ((n - nb) % LANES >= SSM_P)
    kd = k - SSD_B_PARTS * SSM_HEADS
    sel_d = (n >= nb) & (kd >= 0) & (kd < SSD_DT_PARTS * SSM_HEADS) & (kd % SSM_HEADS == hh)
    return jnp.asarray((sel_b | sel_d).astype(np.float32), dtype=BF16)


def _ssd_kernel(nchunks, xbc_ref, z_ref, dt_ref, sel_ref, dtb_ref, alog_ref, dvec_ref, nw_ref, o_ref, st_ref):
    @pl.when(pl.program_id(1) == 0)
    def _():
        st_ref[...] = jnp.zeros_like(st_ref)

    C = SSD_CHUNK
    causal = _tril_ones(C)
    tril = causal.astype(BF16)
    lane = lax.broadcasted_iota(jnp.int32, (C, LANES), 1)
    row = lax.broadcasted_iota(jnp.int32, (C, LANES), 0)
    lo_lane = lane < SSM_P
    lo_row = row < SSM_P
    head_lane = lane < SSM_HEADS
    a_neg = -jnp.exp(alog_ref[...])
    d_inner = SSM_HEADS * SSM_P
    gn = SSM_GROUPS * SSM_N
    hpg = SSM_HEADS // SSM_GROUPS

    def body(c, carry):
        rs = pl.ds(pl.multiple_of(c * C, C), C)
        dt = _softplus(dt_ref[rs, :] + dtb_ref[...])
        b_col = _cumsum_rows(tril, dt * a_neg)
        b_row = b_col.T
        e_last = jnp.exp(b_col[C - 1:C, :])
        d1 = dt.astype(BF16)
        parts = list(_split3(b_col)) + [d1, (dt - d1.astype(F32)).astype(BF16)]
        packed = jnp.zeros((C, LANES), F32)
        for idx, part in enumerate(parts):
            v = jnp.where(head_lane, part.astype(F32), 0.0)
            packed = packed + (pltpu.roll(v, SSM_HEADS * idx, 1) if idx else v)
        bc = _dot(packed.astype(BF16), sel_ref[...])
        for g in range(SSM_GROUPS):
            bg = xbc_ref[rs, d_inner + g * SSM_N:d_inner + (g + 1) * SSM_N]
            cg = xbc_ref[rs, d_inner + gn + g * SSM_N:d_inner + gn + (g + 1) * SSM_N]
            cbm = jnp.where(causal, _dot_nt(cg, bg), 0.0)
            ys = []
            for pp in range(hpg // 2):
                p = g * (hpg // 2) + pp
                ha, hb = 2 * p, 2 * p + 1
                psl = slice(p * LANES, (p + 1) * LANES)
                x2 = xbc_ref[rs, psl].astype(F32)
                b_a = bc[:, ha * LANES:(ha + 1) * LANES]
                b_b = bc[:, hb * LANES:(hb + 1) * LANES]
                dt_sel = bc[:, (SSM_HEADS + p) * LANES:(SSM_HEADS + p + 1) * LANES]
                xdt = x2 * dt_sel
                xdtb = xdt.astype(BF16)

                def wmat(h, b_i):
                    seg = jnp.minimum(b_i - b_row[h:h + 1, :], 0.0)
                    return (jnp.exp(seg) * cbm).astype(BF16)

                y2 = jnp.where(lo_lane, _dot(wmat(ha, b_a), xdtb), _dot(wmat(hb, b_b), xdtb))
                stp = st_ref[p]
                b_sel = jnp.where(lo_lane, b_a, b_b)
                y2 = y2 + _dot_nt(cg, stp.astype(BF16)) * jnp.exp(b_sel)
                upd = _dot_tn((xdt * jnp.exp(b_sel[C - 1:C, :] - b_sel)).astype(BF16), bg)
                dsel = jnp.where(lo_row, e_last[:, ha:ha + 1], e_last[:, hb:hb + 1])
                st_ref[p] = stp * dsel + upd
                y2 = y2 + dvec_ref[:, psl] * x2
                y2 = y2 * _silu(z_ref[rs, psl].astype(F32))
                ys.append(y2)
            yg = jnp.concatenate(ys, axis=-1)
            var = jnp.mean(yg * yg, axis=-1, keepdims=True)
            gsl = slice(g * 2 * LANES, (g + 1) * 2 * LANES)
            o_ref[rs, gsl] = (yg * lax.rsqrt(var + NORM_EPS) * nw_ref[:, gsl]).astype(o_ref.dtype)
        return carry

    lax.fori_loop(0, nchunks, body, 0)


def _ssd(pb, pf, sel, dtb, alog, dvec, nw, layer, bsz, t_len, tb):
    n = bsz * t_len
    tblocks = t_len // tb
    sp = functools.partial(_col_spec, tb, tblocks)
    conv_dim = SSM_HEADS * SSM_P + 2 * SSM_GROUPS * SSM_N
    d_inner = SSM_HEADS * SSM_P
    return pl.pallas_call(
        functools.partial(_ssd_kernel, tb // SSD_CHUNK),
        grid=(bsz, tblocks),
        in_specs=[sp(conv_dim, BF_COLS["sxbc"][0]), sp(d_inner, BF_COLS["sz"][0]), sp(LANES, F_COLS["sdt"][0]),
                  pl.BlockSpec((LANES, SSD_SEL_COLS), lambda b, t: (0, 0)),
                  _row_param(LANES, layer), _row_param(LANES, layer),
                  _row_param(d_inner, layer), _row_param(d_inner, layer)],
        out_specs=pl.BlockSpec((tb, d_inner), lambda b, t: (b * tblocks + t, 0)),
        out_shape=jax.ShapeDtypeStruct((n, d_inner), BF16),
        scratch_shapes=[pltpu.VMEM((SSM_HEADS // 2, 2 * SSM_P, SSM_N), F32)],
        compiler_params=_cparams(("parallel", "arbitrary")),
        name="ssd",
    )(pb, pb, pf, sel, dtb, alog, dvec, nw)


def _merge_kernel(nsub, yg_ref, yh_ref, ys_ref, gates_ref, x_ref, w1_ref, w2_ref, w3_ref, wo_ref,
                  n2_ref, rwh_ref, rwl_ref, rb_ref, xo_ref, h2_ref, ri_ref, cnt_ref, run_ref):
    @pl.when(pl.program_id(0) == 0)
    def _():
        run_ref[...] = jnp.zeros_like(run_ref)

    sub = x_ref.shape[0] // nsub

    def matmul_stages(rs):
        def branch(k, y_ref, w_ref):
            g = jax.nn.sigmoid(gates_ref[rs, k * D_MODEL:(k + 1) * D_MODEL].astype(F32))
            return g * _dot(y_ref[rs, :], w_ref[...])
        m = branch(0, yg_ref, w1_ref)
        yield
        m = m + branch(1, yh_ref, w2_ref)
        yield
        m = m + branch(2, ys_ref, w3_ref)
        yield
        xo_ref[rs, :] = x_ref[rs, :] + _dot(m.astype(BF16), wo_ref[...])
        yield

    routing = iter(())
    for s in range(nsub):
        rs = slice(s * sub, (s + 1) * sub)
        for _ in matmul_stages(rs):
            next(routing, None)
        for _ in routing:
            pass
        routing = _route(xo_ref, n2_ref, rwh_ref, rwl_ref, rb_ref, h2_ref, ri_ref, run_ref, rs)
    for _ in routing:
        pass
    cnt_ref[...] = run_ref[...]


def _route(xo_ref, n2_ref, rwh_ref, rwl_ref, rb_ref, h2_ref, ri_ref, run_ref, rs):
    xn = xo_ref[rs, :]
    h2 = xn * lax.rsqrt(jnp.mean(xn * xn, axis=-1, keepdims=True) + NORM_EPS) * n2_ref[...]
    h2_ref[rs, :] = h2
    yield
    hi = h2.astype(BF16)
    lo = (h2 - hi.astype(F32)).astype(BF16)
    logits = _dot(hi, rwh_ref[...]) + _dot(lo, rwh_ref[...]) + _dot(hi, rwl_ref[...]) + rb_ref[...]
    yield

    lane = lax.broadcasted_iota(jnp.int32, logits.shape, 1)
    lanef = lane.astype(F32)
    neg = jnp.float32(-1e30)
    big = jnp.float32(1e9)
    gmask = lane < MOE_GROUPS
    gl = jnp.where(gmask, logits, neg)
    gexp = jnp.where(gmask, jnp.exp(gl - jnp.max(gl, axis=-1, keepdims=True)), 0.0)
    gprob = gexp / jnp.sum(gexp, axis=-1, keepdims=True)
    g_p = jnp.max(gprob, axis=-1, keepdims=True)
    g_idx = jnp.min(jnp.where(gmask & (gprob == g_p), lanef, big), axis=-1, keepdims=True)
    yield
    lo_l = MOE_GROUPS + EXPERTS_PER_GROUP * g_idx
    emask = (lanef >= lo_l) & (lanef < lo_l + EXPERTS_PER_GROUP)
    el = jnp.where(emask, logits, neg)
    eexp = jnp.where(emask, jnp.exp(el - jnp.max(el, axis=-1, keepdims=True)), 0.0)
    eprob = eexp / jnp.sum(eexp, axis=-1, keepdims=True)
    p1 = jnp.max(jnp.where(emask, eprob, -1.0), axis=-1, keepdims=True)
    i1 = jnp.min(jnp.where(emask & (eprob == p1), lanef, big), axis=-1, keepdims=True)
    rest = emask & (lanef != i1)
    p2 = jnp.max(jnp.where(rest, eprob, -1.0), axis=-1, keepdims=True)
    i2 = jnp.min(jnp.where(rest & (eprob == p2), lanef, big), axis=-1, keepdims=True)
    den = p1 + p2
    w_a = g_p * (p1 / den)
    w_b = g_p * (p2 / den)
    yield

    nrow = logits.shape[0]
    r_i = lax.broadcasted_iota(jnp.int32, (nrow, nrow), 0)
    c_i = lax.broadcasted_iota(jnp.int32, (nrow, nrow), 1)
    before = (r_i > c_i).astype(BF16)
    oh1 = (lanef == i1).astype(F32)
    oh2 = (lanef == i2).astype(F32)
    run = run_ref[...]
    c1 = jnp.sum(oh1, axis=0, keepdims=True)
    pos1 = jnp.sum(oh1 * (run + _dot(before, oh1.astype(BF16))), axis=-1, keepdims=True)
    pos2 = jnp.sum(oh2 * (run + c1 + _dot(before, oh2.astype(BF16))), axis=-1, keepdims=True)
    run_ref[...] = run + c1 + jnp.sum(oh2, axis=0, keepdims=True)

    out = jnp.zeros_like(logits)
    for k, val in enumerate((i1 - MOE_GROUPS, i2 - MOE_GROUPS, w_a, w_b, pos1, pos2)):
        out = jnp.where(lane == k, val, out)
    ri_ref[rs, :] = out


def _merge(yg, yh, ys, pb, x2d, w1, w2, w3, wo, n2, rwh, rwl, rb, layer, tm, nsub=MERGE_SUBTILES):
    n = x2d.shape[0]
    gates_blk = BF_COLS["gates"][0] // BF_COLS["gates"][1]

    def wspec(k):
        return pl.BlockSpec((None, k, D_MODEL), lambda i: (layer, 0, 0))

    return pl.pallas_call(
        functools.partial(_merge_kernel, nsub),
        grid=(n // tm,),
        in_specs=[pl.BlockSpec((tm, MIX_W), lambda i: (i, 0)),
                  pl.BlockSpec((tm, MIX_W), lambda i: (i, 0)),
                  pl.BlockSpec((tm, D_MODEL), lambda i: (i, 0)),
                  pl.BlockSpec((tm, 3 * D_MODEL), lambda i: (i, gates_blk)),
                  pl.BlockSpec((tm, D_MODEL), lambda i: (i, 0)),
                  wspec(MIX_W), wspec(MIX_W), wspec(D_MODEL), wspec(D_MODEL),
                  pl.BlockSpec((None, 1, D_MODEL), lambda i: (layer, 0, 0)),
                  pl.BlockSpec((None, D_MODEL, LANES), lambda i: (layer, 0, 0)),
                  pl.BlockSpec((None, D_MODEL, LANES), lambda i: (layer, 0, 0)),
                  pl.BlockSpec((None, 1, LANES), lambda i: (layer, 0, 0))],
        out_specs=[pl.BlockSpec((tm, D_MODEL), lambda i: (i, 0)),
                   pl.BlockSpec((tm, D_MODEL), lambda i: (i, 0)),
                   pl.BlockSpec((tm, LANES), lambda i: (i, 0)),
                   pl.BlockSpec((1, LANES), lambda i: (0, 0))],
        out_shape=[jax.ShapeDtypeStruct((n, D_MODEL), F32),
                   jax.ShapeDtypeStruct((n, D_MODEL), F32),
                   jax.ShapeDtypeStruct((n, LANES), F32),
                   jax.ShapeDtypeStruct((1, LANES), F32)],
        scratch_shapes=[pltpu.VMEM((1, LANES), F32)],
        compiler_params=_cparams(("arbitrary",)),
        name="merge_router",
    )(yg, yh, ys, pb, x2d, w1, w2, w3, wo, n2, rwh, rwl, rb)


def _dispatch_kernel(tp, nsteps, blk, idx_hbm, pstart_ref, pend_ref, h_ref, xb_out, idx_smem, zbuf, isem, rsem,
                     zsem):
    i = pl.program_id(0)
    slot = i % 2

    def idx_copy(step, s):
        return pltpu.make_async_copy(idx_hbm.at[step], idx_smem.at[s], isem.at[s])

    def row_copy(r, dst):
        return pltpu.make_async_copy(h_ref.at[pl.ds(r, 1)], xb_out.at[pl.ds(dst, 1)], rsem)

    def zero_copy(start):
        if not isinstance(start, int):
            start = pl.multiple_of(start, blk)
        return pltpu.make_async_copy(zbuf, xb_out.at[pl.ds(start, blk)], zsem)

    @pl.when(i == 0)
    def _():
        idx_copy(0, 0).start()
        if nsteps > 1:
            idx_copy(1, 1).start()
        zbuf[...] = jnp.zeros_like(zbuf)
        total = xb_out.shape[0]
        todo = [(pend_ref[e] > pstart_ref[e], pend_ref[e] - blk) for e in range(N_EXPERTS)]
        todo += [(start >= pend_ref[N_EXPERTS - 1], start) for start in range(total - N_EXPERTS * blk, total, blk)]
        for cond, start in todo:
            @pl.when(cond)
            def _():
                zero_copy(start).start()
        for cond, start in todo:
            @pl.when(cond)
            def _():
                zero_copy(start).wait()

    idx_copy(i, slot).wait()

    for r in range(tp):
        for k in range(2):
            row_copy(r, idx_smem[slot, k * tp + r]).start()

    @pl.when(i + 2 < nsteps)
    def _():
        idx_copy(i + 2, slot).start()

    def drain(r, carry):
        for k in range(2):
            row_copy(0, 0).wait()
        return carry

    lax.fori_loop(0, tp, drain, 0, unroll=8)


def _dispatch(dest_tiles, pstart, pend, h2, rows, tp, blk):
    n = h2.shape[0]
    return pl.pallas_call(
        functools.partial(_dispatch_kernel, tp, n // tp, blk),
        grid=(n // tp,),
        in_specs=[pl.BlockSpec(memory_space=pl.ANY),
                  pl.BlockSpec(memory_space=pltpu.SMEM),
                  pl.BlockSpec(memory_space=pltpu.SMEM),
                  pl.BlockSpec((tp, D_MODEL), lambda i: (i, 0))],
        out_specs=pl.BlockSpec(memory_space=pl.ANY),
        out_shape=jax.ShapeDtypeStruct((rows, D_MODEL), F32),
        scratch_shapes=[pltpu.SMEM((2, 2 * tp), jnp.int32),
                        pltpu.VMEM((blk, D_MODEL), F32),
                        pltpu.SemaphoreType.DMA((2,)),
                        pltpu.SemaphoreType.DMA(()),
                        pltpu.SemaphoreType.DMA(())],
        compiler_params=_cparams(("arbitrary",)),
        name="moe_dispatch",
    )(dest_tiles, pstart, pend, h2)


def _ffn_kernel(be_ref, nb_ref, x_ref, wg_ref, wu_ref, wd_ref, o_ref, wg_s, wu_s, wd_s):
    i = pl.program_id(0)
    used = i < nb_ref[0]

    @pl.when(used & ((i == 0) | (be_ref[i] != be_ref[jnp.maximum(i - 1, 0)])))
    def _():
        wg_s[...] = wg_ref[...].astype(BF16)
        wu_s[...] = wu_ref[...].astype(BF16)
        wd_s[...] = wd_ref[...].astype(BF16)

    @pl.when(used)
    def _():
        x = x_ref[...].astype(BF16)
        hid = _silu(_dot(x, wg_s[...])) * _dot(x, wu_s[...])
        o_ref[...] = _dot(hid.astype(BF16), wd_s[...])

    @pl.when(jnp.logical_not(used))
    def _():
        o_ref[...] = jnp.zeros_like(o_ref)


def _ffn(block_e, nblk_used, xb, wg, wu, wd, layer, blk):
    rows = xb.shape[0]

    def x_map(i, be, nb):
        return (jnp.minimum(i, nb[0] - 1), 0)

    def w_map(i, be, nb):
        return (layer, be[jnp.minimum(i, nb[0] - 1)], 0, 0)

    grid_spec = pltpu.PrefetchScalarGridSpec(
        num_scalar_prefetch=2,
        grid=(rows // blk,),
        in_specs=[pl.BlockSpec((blk, D_MODEL), x_map),
                  pl.BlockSpec((None, None, D_MODEL, MOE_FF), w_map),
                  pl.BlockSpec((None, None, D_MODEL, MOE_FF), w_map),
                  pl.BlockSpec((None, None, MOE_FF, D_MODEL), w_map)],
        out_specs=pl.BlockSpec((blk, D_MODEL), lambda i, be, nb: (i, 0)),
        scratch_shapes=[pltpu.VMEM((D_MODEL, MOE_FF), BF16),
                        pltpu.VMEM((D_MODEL, MOE_FF), BF16),
                        pltpu.VMEM((MOE_FF, D_MODEL), BF16)],
    )
    return pl.pallas_call(
        _ffn_kernel,
        grid_spec=grid_spec,
        out_shape=jax.ShapeDtypeStruct((rows, D_MODEL), F32),
        compiler_params=_cparams(("arbitrary",)),
        name="moe_ffn",
    )(block_e, nblk_used, xb, wg, wu, wd)


def _combine_kernel(tc, nsteps, final, idx_hbm, yb_hbm, x_ref, w_ref, fw_ref, o_ref, idx_smem, ybuf, isem, rsem):
    i = pl.program_id(0)
    slot = i % 2

    def idx_copy(step, s):
        return pltpu.make_async_copy(idx_hbm.at[step], idx_smem.at[s], isem.at[s])

    def row_copy(src, r, s):
        return pltpu.make_async_copy(yb_hbm.at[pl.ds(src, 1)], ybuf.at[s, pl.ds(r, 1)], rsem.at[s])

    def issue_rows(s):
        for r in range(2 * tc):
            row_copy(idx_smem[s, r], r, s).start()

    @pl.when(i == 0)
    def _():
        idx_copy(0, 0).start()
        idx_copy(0, 0).wait()
        issue_rows(0)
        if nsteps > 1:
            idx_copy(1, 1).start()

    @pl.when(i + 1 < nsteps)
    def _():
        idx_copy(i + 1, 1 - slot).wait()
        issue_rows(1 - slot)

    @pl.when(i + 2 < nsteps)
    def _():
        idx_copy(i + 2, slot).start()

    def drain(r, carry):
        row_copy(0, 0, slot).wait()
        return carry

    lax.fori_loop(0, 2 * tc, drain, 0, unroll=8)
    w = w_ref[...]
    out = x_ref[...] + w[:, 2:3] * ybuf[slot, 0:tc, :] + w[:, 3:4] * ybuf[slot, tc:2 * tc, :]
    if final:
        out = out * lax.rsqrt(jnp.mean(out * out, axis=-1, keepdims=True) + NORM_EPS) * fw_ref[...]
    o_ref[...] = out


def _combine(dest_tiles, yb, x2d, ri, fw, final, tc):
    n = x2d.shape[0]
    return pl.pallas_call(
        functools.partial(_combine_kernel, tc, n // tc, final),
        grid=(n // tc,),
        in_specs=[pl.BlockSpec(memory_space=pl.ANY),
                  pl.BlockSpec(memory_space=pl.ANY),
                  pl.BlockSpec((tc, D_MODEL), lambda i: (i, 0)),
                  pl.BlockSpec((tc, LANES), lambda i: (i, 0)),
                  pl.BlockSpec((1, D_MODEL), lambda i: (0, 0))],
        out_specs=pl.BlockSpec((tc, D_MODEL), lambda i: (i, 0)),
        out_shape=jax.ShapeDtypeStruct((n, D_MODEL), F32),
        scratch_shapes=[pltpu.SMEM((2, 2 * tc), jnp.int32),
                        pltpu.VMEM((2, 2 * tc, D_MODEL), F32),
                        pltpu.SemaphoreType.DMA((2,)),
                        pltpu.SemaphoreType.DMA((2,))],
        compiler_params=_cparams(("arbitrary",)),
        name="moe_combine",
    )(dest_tiles, yb, x2d, ri, fw)


def _route_slots(ri, cnt, blk):
    n = ri.shape[0]
    e = ri[:, 0:2].astype(jnp.int32)
    pos = ri[:, 4:6].astype(jnp.int32)
    counts = cnt[0, MOE_GROUPS:MOE_GROUPS + N_EXPERTS].astype(jnp.int32)
    padded = (counts + blk - 1) // blk * blk
    pend = jnp.cumsum(padded)
    pstart = pend - padded
    experts = jnp.arange(N_EXPERTS, dtype=jnp.int32)
    dest = (jnp.sum(jnp.where(e[:, :, None] == experts, pstart, 0), axis=-1) + pos).astype(jnp.int32)
    rows = 2 * n + N_EXPERTS * blk
    nblk = rows // blk
    block_start = jnp.arange(nblk, dtype=jnp.int32) * blk
    block_e = jnp.minimum(jnp.sum((block_start[:, None] >= pend[None, :]).astype(jnp.int32), axis=1),
                          N_EXPERTS - 1).astype(jnp.int32)
    nblk_used = (pend[-1] // blk).astype(jnp.int32).reshape(1)
    return dest, block_e, nblk_used, pstart.astype(jnp.int32), pend.astype(jnp.int32), rows


def _tile_slots(dest, tile):
    n = dest.shape[0]
    return dest.reshape(n // tile, tile, 2).transpose(0, 2, 1).reshape(n // tile, 2 * tile)


def _pad_heads(w, heads, width):
    lead = w.shape[:-1]
    w = w.reshape(lead + (heads, width))
    w = jnp.pad(w, [(0, 0)] * len(lead) + [(0, 0), (0, HEAD_W - width)])
    return w.reshape(lead + (heads * HEAD_W,))


def _pad_last(w, width):
    return jnp.pad(w, [(0, 0)] * (w.ndim - 1) + [(0, width - w.shape[-1])])


def _prep_inproj(w_in):
    gla_kw, gla_vw, hw = HEADS * GLA_DK, MIX_W, MIX_W
    d_inner = SSM_HEADS * SSM_P
    conv_dim = d_inner + 2 * SSM_GROUPS * SSM_N
    sizes = (gla_kw, gla_kw, gla_vw, GLA_RANK, gla_vw, hw, hw, hw, hw, d_inner, conv_dim, SSM_HEADS, 3 * D_MODEL)
    offs = [0]
    for s in sizes:
        offs.append(offs[-1] + s)
    seg = {name: w_in[..., offs[i]:offs[i + 1]] for i, name in enumerate(
        ("gq", "gk", "gv", "glow", "gr", "hq", "hf", "hi", "hg", "sz", "sxbc", "sdt", "gates"))}
    seg["gq"] = _pad_heads(seg["gq"], HEADS, GLA_DK)
    seg["gk"] = _pad_heads(seg["gk"], HEADS, GLA_DK)
    seg["glow"] = _pad_last(seg["glow"], LANES)
    seg["sdt"] = _pad_last(seg["sdt"], LANES)
    wb = jnp.concatenate([seg[k] for k in sorted(BF_COLS, key=lambda k: BF_COLS[k][0])], axis=-1).astype(BF16)
    wf = jnp.concatenate([seg[k] for k in sorted(F_COLS, key=lambda k: F_COLS[k][0])], axis=-1).astype(BF16)
    return wb, wf


def kernel(x, norm1_w, w_in, gla_gk_up, gla_gk_bias, gla_norm_w, hgrn_lb_logits, hgrn_norm_w,
           ssm_conv_w, ssm_conv_b, ssm_dt_bias, ssm_a_log, ssm_d, ssm_norm_w,
           w_br_gla, w_br_hgrn, w_br_ssm, w_out, norm2_w,
           router_group_w, router_group_b, router_expert_w, router_expert_b,
           moe_w_gate, moe_w_up, moe_w_down, final_norm_w):
    bsz, t_len, d = x.shape
    n = bsz * t_len
    depth = w_in.shape[0]
    x2d = x.reshape(n, d).astype(F32)

    wb, wf = _prep_inproj(w_in)
    row = lambda a: a.astype(F32)[:, None, :]
    norm1 = row(norm1_w)
    norm2 = row(norm2_w)
    up = _pad_heads(jnp.pad(gla_gk_up, ((0, 0), (0, LANES - GLA_RANK), (0, 0))), HEADS, GLA_DK).astype(BF16)
    gb = row(_pad_heads(gla_gk_bias, HEADS, GLA_DK))
    gnw = row(gla_norm_w)
    lower = jnp.cumsum(jax.nn.softmax(hgrn_lb_logits.astype(F32), axis=0), axis=0)
    lb = lower - lower[0:1]
    loglb, log1m, onem = row(jnp.log(lb)), row(jnp.log1p(-lb)), row(1.0 - lb)
    hnw = row(hgrn_norm_w)
    cw = ssm_conv_w.astype(F32)
    cb = row(ssm_conv_b)
    dtb = row(_pad_last(ssm_dt_bias, LANES))
    alog = row(_pad_last(ssm_a_log, LANES))
    dvec = row(jnp.repeat(ssm_d, SSM_P, axis=-1))
    snw = row(ssm_norm_w)
    sel = _ssd_select()
    w1, w2, w3, wo = (w.astype(BF16) for w in (w_br_gla, w_br_hgrn, w_br_ssm, w_out))
    rw = _pad_last(jnp.concatenate([router_group_w, router_expert_w], axis=-1).astype(F32), LANES)
    rwh = rw.astype(BF16)
    rwl = (rw - rwh.astype(F32)).astype(BF16)
    rb = row(_pad_last(jnp.concatenate([router_group_b, router_expert_b], axis=-1), LANES))
    fw = final_norm_w.astype(F32)[None, :]

    for l in range(depth):
        pb, pf = _inproj(x2d, norm1, wb, wf, cw, cb, l, t_len, INPROJ_TILE)
        yg = _lin_attn("gla", pb, pf, (up, gb, gnw), l, bsz, t_len, LA_TILE)
        yh = _lin_attn("hgrn", pb, pf, (loglb, log1m, onem, hnw), l, bsz, t_len, LA_TILE)
        ys = _ssd(pb, pf, sel, dtb, alog, dvec, snw, l, bsz, t_len, SSD_TILE)
        x2d, h2, ri, cnt = _merge(yg, yh, ys, pb, x2d, w1, w2, w3, wo, norm2, rwh, rwl, rb, l, MERGE_TILE)
        dest, block_e, nblk_used, pstart, pend, rows = _route_slots(ri, cnt, MOE_BLOCK)
        xb = _dispatch(_tile_slots(dest, DISPATCH_TILE), pstart, pend, h2, rows, DISPATCH_TILE, MOE_BLOCK)
        yb = _ffn(block_e, nblk_used, xb, moe_w_gate, moe_w_up, moe_w_down, l, MOE_BLOCK)
        x2d = _combine(_tile_slots(dest, COMBINE_TILE), yb, x2d, ri, fw, l == depth - 1, COMBINE_TILE)
    return x2d.reshape(bsz, t_len, d)
```
